```python
import math
import jax
import jax.numpy as jnp
from jax import lax
import numpy as np

D_MODEL = 1024
BATCH = 16
SEQ = 256
DEPTH = 4
DEC_BATCH = 8
DEC_SEQ = 2048
PAST_LEN = 256

GRID_W = 64
EPS = 1e-6
POOL_WIDTH = D_MODEL // 4
MLA_WIDTH = D_MODEL // 2
GMLP_WIDTH = D_MODEL // 4
MIX_WIDTH = POOL_WIDTH + MLA_WIDTH + GMLP_WIDTH
POOL_WINDOWS = (2, 4, 8, 16)
POOL_GROUPS = len(POOL_WINDOWS)
POOL_GDIM = POOL_WIDTH // POOL_GROUPS
QK_NOPE = 128
QK_ROPE = 64
V_DIM = 128
MLA_HEADS = MLA_WIDTH // V_DIM
Q_RANK = 3 * D_MODEL // 8
KV_RANK = D_MODEL // 4
ROPE_THETA = 10000.0
ATTN_BLOCK = 128
CHUNK = 128
GMLP_GROUPS = 4
GMLP_GDIM = GMLP_WIDTH // GMLP_GROUPS
D_FF = 4 * D_MODEL
OFF_Q = POOL_WIDTH
OFF_KV = OFF_Q + Q_RANK
OFF_R = OFF_KV + KV_RANK
OFF_G = OFF_R + QK_ROPE
IN_WIDTH = OFF_G + 2 * GMLP_WIDTH

kernel_name = 'hybrid_pool_mla_gmlp_diffusion_step'


def rmsnorm(x, g):
    xf = x.astype(jnp.float32)
    y = xf * lax.rsqrt(jnp.mean(xf * xf, axis=-1, keepdims=True) + EPS)
    return (y * g.astype(jnp.float32)).astype(x.dtype)


def pool_mix(h, w_pool, pool_scale):
    B, L, _ = h.shape
    hf = h.astype(jnp.float32)
    csum = jnp.concatenate([jnp.zeros((B, 1, POOL_WIDTH), jnp.float32),
                            lax.cumsum(hf, axis=1)], axis=1)
    t = jnp.arange(L)
    groups = []
    for gi, w in enumerate(POOL_WINDOWS):
        lo = jnp.clip(t - w // 2, 0, L)
        hi = jnp.clip(t + (w - w // 2), 0, L)
        cnt = (hi - lo).astype(jnp.float32)[None, :, None]
        sl = slice(gi * POOL_GDIM, (gi + 1) * POOL_GDIM)
        win = csum[:, hi, sl] - csum[:, lo, sl]
        groups.append(win / cnt - hf[:, :, sl])
    pooled = jnp.stack(groups, axis=2)
    y = jnp.einsum('blgc,gcd->blgd', pooled, w_pool.astype(jnp.float32)).reshape(B, L, POOL_WIDTH)
    return (y * pool_scale.astype(jnp.float32)).astype(h.dtype)


def axial_rope(L):
    rows = L // GRID_W
    row = jnp.repeat(jnp.arange(rows, dtype=jnp.float32), GRID_W)
    col = jnp.tile(jnp.arange(GRID_W, dtype=jnp.float32), rows)
    n_freq = QK_ROPE // 4
    inv = 1.0 / (ROPE_THETA ** (jnp.arange(n_freq, dtype=jnp.float32) / n_freq))
    ang = jnp.concatenate([row[:, None] * inv, col[:, None] * inv], axis=-1)
    return jnp.cos(ang), jnp.sin(ang)


def apply_rope(x, cos, sin):
    xf = x.astype(jnp.float32)
    x1, x2 = xf[..., :QK_ROPE // 2], xf[..., QK_ROPE // 2:]
    return jnp.concatenate([x1 * cos - x2 * sin, x2 * cos + x1 * sin], axis=-1).astype(x.dtype)


def expand_kv(ckv, w_ukv):
    B, L, _ = ckv.shape
    kv = (ckv @ w_ukv).reshape(B, L, MLA_HEADS, QK_NOPE + V_DIM)
    return kv[..., :QK_NOPE], kv[..., QK_NOPE:]


def block_attention(qn, qr, kn, kr, v):
    B, Lq, H, _ = qn.shape
    nb = Lq // ATTN_BLOCK
    scale = 1.0 / math.sqrt(QK_NOPE + QK_ROPE)

    def to_blocks(a):
        return a.reshape((B, nb, ATTN_BLOCK) + a.shape[2:]).swapaxes(0, 1)

    def one_block(qs):
        qn_b, qr_b = qs
        s = (jnp.einsum('bqhd,bkhd->bhqk', qn_b, kn, preferred_element_type=jnp.float32)
             + jnp.einsum('bqhr,bkr->bhqk', qr_b, kr, preferred_element_type=jnp.float32)) * scale
        p = jax.nn.softmax(s, axis=-1).astype(v.dtype)
        return jnp.einsum('bhqk,bkhd->bqhd', p, v)

    o = lax.map(one_block, (to_blocks(qn), to_blocks(qr)))
    return o.swapaxes(0, 1).reshape(B, Lq, H * V_DIM)


def chunk_gmlp(u, v, w_s, b_s):
    B, L, _ = u.shape
    n = L // CHUNK
    vr = v.reshape(B, n, CHUNK, GMLP_GROUPS, GMLP_GDIM)
    mixed = jnp.einsum('gpq,bnqgc->bnpgc', w_s, vr) + b_s.T[:, :, None]
    return u * mixed.reshape(B, L, GMLP_WIDTH)


def mixer(hn, p, rope, ctx):
    B, L, _ = hn.shape
    proj = hn @ p['w_in']
    hp = proj[..., :OFF_Q]
    cq = proj[..., OFF_Q:OFF_KV]
    ckv_raw = proj[..., OFF_KV:OFF_R]
    kr = proj[..., OFF_R:OFF_G]
    uv = proj[..., OFF_G:]
    y_pool = pool_mix(hp, p['w_pool'], p['pool_scale'])
    q = (rmsnorm(cq, p['g_q']) @ p['w_uq']).reshape(B, L, MLA_HEADS, QK_NOPE + QK_ROPE)
    qn, qr = q[..., :QK_NOPE], q[..., QK_NOPE:]
    ckv = rmsnorm(ckv_raw, p['g_kv'])
    kn, v = expand_kv(ckv, p['w_ukv'])
    kr_keys = kr
    if rope is not None:
        cos, sin = rope
        qr = apply_rope(qr, cos[:, None, :], sin[:, None, :])
        kr_keys = apply_rope(kr, cos, sin)
    if ctx is not None:
        ckv_c, kr_c = ctx
        kn_c, v_c = expand_kv(ckv_c, p['w_ukv'])
        kn = jnp.concatenate([kn_c, kn], axis=1)
        v = jnp.concatenate([v_c, v], axis=1)
        kr_keys = jnp.concatenate([kr_c, kr_keys], axis=1)
    y_mla = block_attention(qn, qr, kn, kr_keys, v)
    uv = jax.nn.gelu(uv)
    u, vg = uv[..., :GMLP_WIDTH], uv[..., GMLP_WIDTH:]
    y_g = chunk_gmlp(u, rmsnorm(vg, p['g_sgu']), p['w_s'], p['b_s'])
    y = jnp.concatenate([y_pool, y_mla, y_g], axis=-1) @ p['w_out']
    return y, ckv, kr


def trunk_layer(x, mod, p, rope, ctx):
    sh1, sc1, g1, sh2, sc2, g2 = jnp.split(mod, 6, axis=-1)
    h = rmsnorm(x, p['g_mix']) * (1 + sc1) + sh1
    y, ckv, kr = mixer(h, p, rope, ctx)
    x = x + g1 * y
    h = rmsnorm(x, p['g_ffn']) * (1 + sc2) + sh2
    f = jnp.square(jax.nn.relu(h @ p['w_ff1'])) @ p['w_ff2']
    return x + g2 * f, ckv, kr


def setup_inputs(seed: int = 0) -> dict:
    key = jax.random.key(seed)
    ks = jax.random.split(key, 24)
    f32 = jnp.float32

    def nrm(k, shape, scale=1.0):
        return jax.random.normal(k, shape, f32) * scale

    def gain(k, shape):
        return 1.0 + 0.02 * jax.random.normal(k, shape, f32)

    return {
        'x_prompt': nrm(ks[0], (BATCH, SEQ, D_MODEL)),
        'x_sample': nrm(ks[1], (DEC_BATCH, DEC_SEQ, D_MODEL)),
        'cache_ckv': nrm(ks[2], (DEC_BATCH, DEPTH, PAST_LEN, KV_RANK)),
        'cache_krope': nrm(ks[3], (DEC_BATCH, DEPTH, PAST_LEN, QK_ROPE)),
        'c': nrm(ks[4], (DEC_BATCH, D_MODEL)),
        'c_ctx': nrm(ks[5], (D_MODEL,)),
        'w_ada': nrm(ks[6], (DEPTH, D_MODEL, 6 * D_MODEL), 0.5 * D_MODEL ** -0.5),
        'b_ada': nrm(ks[7], (DEPTH, 6 * D_MODEL), 0.01),
        'g_mix': gain(ks[8], (DEPTH, D_MODEL)),
        'w_in': nrm(ks[9], (DEPTH, D_MODEL, IN_WIDTH), D_MODEL ** -0.5),
        'w_pool': nrm(ks[10], (DEPTH, POOL_GROUPS, POOL_GDIM, POOL_GDIM), POOL_GDIM ** -0.5),
        'pool_scale': gain(ks[11], (DEPTH, POOL_WIDTH)),
        'g_q': gain(ks[12], (DEPTH, Q_RANK)),
        'w_uq': nrm(ks[13], (DEPTH, Q_RANK, MLA_HEADS * (QK_NOPE + QK_ROPE)), Q_RANK ** -0.5),
        'g_kv': gain(ks[14], (DEPTH, KV_RANK)),
        'w_ukv': nrm(ks[15], (DEPTH, KV_RANK, MLA_HEADS * (QK_NOPE + V_DIM)), KV_RANK ** -0.5),
        'g_sgu': gain(ks[16], (DEPTH, GMLP_WIDTH)),
        'w_s': nrm(ks[17], (DEPTH, GMLP_GROUPS, CHUNK, CHUNK), CHUNK ** -0.5),
        'b_s': nrm(ks[18], (DEPTH, GMLP_GROUPS, CHUNK), 0.01),
        'w_out': nrm(ks[19], (DEPTH, MIX_WIDTH, D_MODEL), MIX_WIDTH ** -0.5),
        'g_ffn': gain(ks[20], (DEPTH, D_MODEL)),
        'w_ff1': nrm(ks[21], (DEPTH, D_MODEL, D_FF), D_MODEL ** -0.5),
        'w_ff2': nrm(ks[22], (DEPTH, D_FF, D_MODEL), D_FF ** -0.5),
        'g_final': gain(ks[23], (D_MODEL,)),
    }


def reference(x_prompt, x_sample, cache_ckv, cache_krope, c, c_ctx, w_ada, b_ada, g_mix,
              w_in, w_pool, pool_scale, g_q, w_uq, g_kv, w_ukv, g_sgu, w_s, b_s, w_out,
              g_ffn, w_ff1, w_ff2, g_final):
    rope = axial_rope(x_sample.shape[1])
    s_ctx = jax.nn.silu(c_ctx)
    s_lat = jax.nn.silu(c)
    xp, xs = x_prompt, x_sample
    ckv_list, kr_list = [], []
    for l in range(DEPTH):
        p = {
            'g_mix': g_mix[l], 'w_in': w_in[l], 'w_pool': w_pool[l], 'pool_scale': pool_scale[l],
            'g_q': g_q[l], 'w_uq': w_uq[l], 'g_kv': g_kv[l], 'w_ukv': w_ukv[l],
            'g_sgu': g_sgu[l], 'w_s': w_s[l], 'b_s': b_s[l], 'w_out': w_out[l],
            'g_ffn': g_ffn[l], 'w_ff1': w_ff1[l], 'w_ff2': w_ff2[l],
        }
        mod_ctx = (s_ctx @ w_ada[l] + b_ada[l])[None, None, :]
        mod_lat = (s_lat @ w_ada[l] + b_ada[l])[:, None, :]
        xp, ckv, kr = trunk_layer(xp, mod_ctx, p, None, None)
        ckv_list.append(ckv)
        kr_list.append(kr)
        xs, _, _ = trunk_layer(xs, mod_lat, p, rope, (cache_ckv[:, l], cache_krope[:, l]))
    y_prompt = rmsnorm(xp, g_final)
    y_sample = rmsnorm(xs, g_final)
    state_ckv = jnp.stack(ckv_list, axis=1)
    state_krope = jnp.stack(kr_list, axis=1)
    return (y_prompt, y_sample, state_ckv, state_krope)
```

```python
import functools
import math

import jax
import jax.numpy as jnp
from jax.experimental import pallas as pl
from jax.experimental.pallas import tpu as pltpu

D = 1024
DEPTH = 4
GRID_W = 64
EPS = 1e-6
POOL_W = 256
POOL_WINDOWS = (2, 4, 8, 16)
POOL_GD = 64
HALO = 8
QK_NOPE = 128
QK_ROPE = 64
V_DIM = 128
HEADS = 4
Q_RANK = 384
KV_RANK = 256
ROPE_THETA = 10000.0
CHUNK = 128
GMLP_W = 256
GMLP_G = 4
GMLP_GD = 64
D_FF = 4096
FF_CHUNK = 1024
P_Q = 256
P_KV = P_Q + Q_RANK
P_R = P_KV + KV_RANK
P_G = P_R + 128
P_END = P_G + 2 * GMLP_W
QK_PAD = 256
SM_SCALE = 1.0 / math.sqrt(QK_NOPE + QK_ROPE)

F32 = jnp.float32
BF16 = jnp.bfloat16
MIB = 1024 * 1024


def _params(vmem_mib):
    return pltpu.CompilerParams(
        dimension_semantics=("arbitrary", "arbitrary"),
        vmem_limit_bytes=vmem_mib * MIB,
    )


def _rms(x, g):
    y = x * jax.lax.rsqrt(jnp.mean(x * x, axis=-1, keepdims=True) + EPS)
    return y * g


def _dot(a, b):
    return jnp.dot(a, b, preferred_element_type=F32)


def _ada_kernel(c_ref, w_ref, b_ref, o_ref):
    c = c_ref[...]
    s = c / (1.0 + jnp.exp(-c))
    o_ref[0] = _dot(s.astype(BF16), w_ref[0].astype(BF16)) + b_ref[0]


def _ada(c_all, w_ada, b_ada):
    rows = c_all.shape[0]
    tn = 1536
    return pl.pallas_call(
        _ada_kernel,
        grid=(DEPTH, 6 * D // tn),
        in_specs=[
            pl.BlockSpec((rows, D), lambda l, j: (0, 0)),
            pl.BlockSpec((1, D, tn), lambda l, j: (l, 0, j)),
            pl.BlockSpec((1, 1, tn), lambda l, j: (l, 0, j)),
        ],
        out_specs=pl.BlockSpec((1, rows, tn), lambda l, j: (l, 0, j)),
        out_shape=jax.ShapeDtypeStruct((DEPTH, rows, 6 * D), F32),
        compiler_params=_params(32),
        name="ada",
    )(c_all, w_ada, b_ada.reshape(DEPTH, 1, 6 * D))


def _cache_kernel(ckv_ref, kr_ref, w_ref, k_ref, v_ref):
    kv = _dot(ckv_ref[0, 0].astype(BF16), w_ref[0])
    kr = kr_ref[0, 0]
    for h in range(HEADS):
        base = h * (QK_NOPE + V_DIM)
        k_ref[0, 0, h] = jnp.concatenate([kv[:, base:base + QK_NOPE].astype(BF16), kr], axis=1)
        v_ref[0, 0, h] = kv[:, base + QK_NOPE:base + QK_NOPE + V_DIM].astype(BF16)


def _cache_expand(cache_ckv, cache_kr_pad, w_ukv):
    b, depth, p, _ = cache_ckv.shape
    return pl.pallas_call(
        _cache_kernel,
        grid=(b, depth),
        in_specs=[
            pl.BlockSpec((1, 1, p, KV_RANK), lambda i, l: (i, l, 0, 0)),
            pl.BlockSpec((1, 1, p, 128), lambda i, l: (i, l, 0, 0)),
            pl.BlockSpec((1, KV_RANK, HEADS * 256), lambda i, l: (l, 0, 0)),
        ],
        out_specs=[
            pl.BlockSpec((1, 1, HEADS, p, QK_PAD), lambda i, l: (i, l, 0, 0, 0)),
            pl.BlockSpec((1, 1, HEADS, p, V_DIM), lambda i, l: (i, l, 0, 0, 0)),
        ],
        out_shape=[
            jax.ShapeDtypeStruct((b, depth, HEADS, p, QK_PAD), BF16),
            jax.ShapeDtypeStruct((b, depth, HEADS, p, V_DIM), BF16),
        ],
        compiler_params=_params(32),
        name="cache_expand",
    )(cache_ckv, cache_kr_pad, w_ukv)


def _shift_up(a, k):
    return pltpu.roll(a, a.shape[0] - k, axis=0)


def _rope(x, c, s1, s2):
    return x * c + pltpu.roll(x, 96, axis=1) * s1 + pltpu.roll(x, 32, axis=1) * s2


def _pre_kernel(*refs, tm, seq, use_rope, write_state):
    it = iter(refs)
    x_ref, xp_ref, xn_ref, mod_ref = next(it), next(it), next(it), next(it)
    gmix_ref, win_ref = next(it), next(it)
    gq_ref, wuq_ref, gkv_ref, wukv_ref = next(it), next(it), next(it), next(it)
    wpool_ref, pscale_ref = next(it), next(it)
    gsgu_ref, ws_ref, bs_ref = next(it), next(it), next(it)
    if use_rope:
        rc_ref, rs1_ref, rs2_ref = next(it), next(it), next(it)
    q_ref, k_ref, v_ref, ypg_ref = next(it), next(it), next(it), next(it)
    if write_state:
        sckv_ref, skr_ref = next(it), next(it)

    i = pl.program_id(1)
    rows = tm + 2 * HALO
    xe = jnp.concatenate([xp_ref[0], x_ref[0], xn_ref[0]], axis=0)
    sh1 = mod_ref[0, :, 0:D]
    sc1 = mod_ref[0, :, D:2 * D]
    h = _rms(xe, gmix_ref[...]) * (1.0 + sc1) + sh1
    proj_e = _dot(h.astype(BF16), win_ref[...])
    proj = proj_e[HALO:HALO + tm]

    pos = i * tm - HALO + jax.lax.broadcasted_iota(jnp.int32, (rows, 1), 0)
    hp_e = jnp.where((pos >= 0) & (pos < seq), proj_e[:, 0:POOL_W], 0.0)
    a2 = hp_e + _shift_up(hp_e, 1)
    a4 = a2 + _shift_up(a2, 2)
    a8 = a4 + _shift_up(a4, 4)
    a16 = a8 + _shift_up(a8, 8)
    win2 = _shift_up(a2, HALO - 1)[:tm]
    win4 = _shift_up(a4, HALO - 2)[:tm]
    win8 = _shift_up(a8, HALO - 4)[:tm]
    win16 = a16[:tm]
    grp = jax.lax.broadcasted_iota(jnp.int32, (1, POOL_W), 1) // POOL_GD
    win = jnp.where(grp == 0, win2, jnp.where(grp == 1, win4, jnp.where(grp == 2, win8, win16)))
    half = jnp.where(grp == 0, 1, jnp.where(grp == 1, 2, jnp.where(grp == 2, 4, 8)))
    t = i * tm + jax.lax.broadcasted_iota(jnp.int32, (tm, 1), 0)
    cnt = jnp.clip(t + half, 0, seq) - jnp.clip(t - half, 0, seq)
    pooled = win / cnt.astype(F32) - proj[:, 0:POOL_W]
    y_pool = _dot(pooled.astype(BF16), wpool_ref[...]) * pscale_ref[...]

    cq = _rms(proj[:, P_Q:P_KV], gq_ref[...])
    q = _dot(cq.astype(BF16), wuq_ref[...])
    ckv = _rms(proj[:, P_KV:P_R], gkv_ref[...])
    kv = _dot(ckv.astype(BF16), wukv_ref[...])
    kr = proj[:, P_R:P_G]
    if write_state:
        sckv_ref[0] = ckv
        skr_ref[0] = kr[:, 0:QK_ROPE]
    if use_rope:
        rc, rs1, rs2 = rc_ref[...], rs1_ref[...], rs2_ref[...]
        kr = _rope(kr, rc, rs1, rs2)
    kr_b = kr.astype(BF16)
    for hd in range(HEADS):
        qn = q[:, hd * QK_PAD:hd * QK_PAD + QK_NOPE]
        qr = q[:, hd * QK_PAD + QK_NOPE:(hd + 1) * QK_PAD]
        if use_rope:
            qr = _rope(qr, rc, rs1, rs2)
        q_ref[0, hd] = (jnp.concatenate([qn, qr], axis=1) * SM_SCALE).astype(BF16)
        base = hd * (QK_NOPE + V_DIM)
        k_ref[0, hd] = jnp.concatenate([kv[:, base:base + QK_NOPE].astype(BF16), kr_b], axis=1)
        v_ref[0, hd] = kv[:, base + QK_NOPE:base + QK_NOPE + V_DIM].astype(BF16)

    uv = jax.nn.gelu(proj[:, P_G:P_END])
    u = uv[:, 0:GMLP_W]
    vg = _rms(uv[:, GMLP_W:], gsgu_ref[...]).astype(BF16)
    ggrp = jax.lax.broadcasted_iota(jnp.int32, (1, GMLP_W), 1) // GMLP_GD
    ys = []
    for cix in range(tm // CHUNK):
        r = _dot(ws_ref[...], vg[cix * CHUNK:(cix + 1) * CHUNK])
        mixed = jnp.where(
            ggrp == 0, r[0:CHUNK],
            jnp.where(ggrp == 1, r[CHUNK:2 * CHUNK],
                      jnp.where(ggrp == 2, r[2 * CHUNK:3 * CHUNK], r[3 * CHUNK:4 * CHUNK])))
        ys.append(u[cix * CHUNK:(cix + 1) * CHUNK] * (mixed + bs_ref[...]))
    y_g = jnp.concatenate(ys, axis=0)
    ypg_ref[0] = jnp.concatenate([y_pool, y_g], axis=1).astype(BF16)


def _pre(x, mod, mod_off, lw, rope_tabs, *, tm, write_state):
    b, seq, _ = x.shape
    nt = seq // tm
    use_rope = rope_tabs is not None
    hb = tm // HALO
    nhb = seq // HALO

    def const(shape):
        return pl.BlockSpec(shape, lambda bi, i: (0,) * len(shape))

    in_specs = [
        pl.BlockSpec((1, tm, D), lambda bi, i: (bi, i, 0)),
        pl.BlockSpec((1, HALO, D), lambda bi, i: (bi, jnp.maximum(i * hb - 1, 0), 0)),
        pl.BlockSpec((1, HALO, D), lambda bi, i: (bi, jnp.minimum((i + 1) * hb, nhb - 1), 0)),
        pl.BlockSpec((1, 1, 6 * D), lambda bi, i: (bi * mod_off[1] + mod_off[0], 0, 0)),
        const((1, D)), const((D, P_END)),
        const((1, Q_RANK)), const((Q_RANK, HEADS * QK_PAD)),
        const((1, KV_RANK)), const((KV_RANK, HEADS * 256)),
        const((POOL_W, POOL_W)), const((1, POOL_W)),
        const((1, GMLP_W)), const((GMLP_G * CHUNK, CHUNK)), const((CHUNK, GMLP_W)),
    ]
    args = [x, x, x, mod, lw["g_mix"], lw["w_in"], lw["g_q"], lw["w_uq"], lw["g_kv"], lw["w_ukv"],
            lw["w_pool"], lw["pool_scale"], lw["g_sgu"], lw["w_s"], lw["b_s"]]
    if use_rope:
        in_specs += [pl.BlockSpec((tm, 128), lambda bi, i: (i, 0))] * 3
        args += list(rope_tabs)
    out_specs = [
        pl.BlockSpec((1, HEADS, tm, QK_PAD), lambda bi, i: (bi, 0, i, 0)),
        pl.BlockSpec((1, HEADS, tm, QK_PAD), lambda bi, i: (bi, 0, i, 0)),
        pl.BlockSpec((1, HEADS, tm, V_DIM), lambda bi, i: (bi, 0, i, 0)),
        pl.BlockSpec((1, tm, 2 * GMLP_W), lambda bi, i: (bi, i, 0)),
    ]
    out_shape = [
        jax.ShapeDtypeStruct((b, HEADS, seq, QK_PAD), BF16),
        jax.ShapeDtypeStruct((b, HEADS, seq, QK_PAD), BF16),
        jax.ShapeDtypeStruct((b, HEADS, seq, V_DIM), BF16),
        jax.ShapeDtypeStruct((b, seq, 2 * GMLP_W), BF16),
    ]
    if write_state:
        out_specs += [
            pl.BlockSpec((1, tm, KV_RANK), lambda bi, i: (bi, i, 0)),
            pl.BlockSpec((1, tm, QK_ROPE), lambda bi, i: (bi, i, 0)),
        ]
        out_shape += [
            jax.ShapeDtypeStruct((b, seq, KV_RANK), F32),
            jax.ShapeDtypeStruct((b, seq, QK_ROPE), F32),
        ]
    return pl.pallas_call(
        functools.partial(_pre_kernel, tm=tm, seq=seq, use_rope=use_rope, write_state=write_state),
        grid=(b, nt),
        in_specs=in_specs,
        out_specs=out_specs,
        out_shape=out_shape,
        compiler_params=_params(48),
        name="pre_lat" if use_rope else "pre_ctx",
    )(*args)


def _attn_kernel(*refs, with_cache):
    if with_cache:
        q_ref, k_ref, v_ref, kc_ref, vc_ref, o_ref = refs
    else:
        q_ref, k_ref, v_ref, o_ref = refs
    nt = (((1,), (1,)), ((), ()))
    for h in range(HEADS):
        q = q_ref[0, h]
        s = jax.lax.dot_general(q, k_ref[0, h], nt, preferred_element_type=F32)
        m = jnp.max(s, axis=-1, keepdims=True)
        if with_cache:
            sc = jax.lax.dot_general(q, kc_ref[0, 0, h], nt, preferred_element_type=F32)
            m = jnp.maximum(m, jnp.max(sc, axis=-1, keepdims=True))
            pc = jnp.exp(sc - m)
        p = jnp.exp(s - m)
        den = jnp.sum(p, axis=-1, keepdims=True)
        o = _dot(p.astype(BF16), v_ref[0, h])
        if with_cache:
            den = den + jnp.sum(pc, axis=-1, keepdims=True)
            o = o + _dot(pc.astype(BF16), vc_ref[0, 0, h])
        o_ref[0, :, h * V_DIM:(h + 1) * V_DIM] = (o / den).astype(BF16)


def _attn(q, k, v, cache, layer, *, tq):
    b, _, seq, _ = q.shape
    with_cache = cache is not None
    in_specs = [
        pl.BlockSpec((1, HEADS, tq, QK_PAD), lambda bi, i: (bi, 0, i, 0)),
        pl.BlockSpec((1, HEADS, seq, QK_PAD), lambda bi, i: (bi, 0, 0, 0)),
        pl.BlockSpec((1, HEADS, seq, V_DIM), lambda bi, i: (bi, 0, 0, 0)),
    ]
    args = [q, k, v]
    if with_cache:
        kc, vc = cache
        p = kc.shape[3]
        in_specs += [
            pl.BlockSpec((1, 1, HEADS, p, QK_PAD), lambda bi, i: (bi, layer, 0, 0, 0)),
            pl.BlockSpec((1, 1, HEADS, p, V_DIM), lambda bi, i: (bi, layer, 0, 0, 0)),
        ]
        args += [kc, vc]
    return pl.pallas_call(
        functools.partial(_attn_kernel, with_cache=with_cache),
        grid=(b, seq // tq),
        in_specs=in_specs,
        out_specs=pl.BlockSpec((1, tq, HEADS * V_DIM), lambda bi, i: (bi, i, 0)),
        out_shape=jax.ShapeDtypeStruct((b, seq, HEADS * V_DIM), BF16),
        compiler_params=_params(48),
        name="attn_lat" if with_cache else "attn_ctx",
    )(*args)


def _post_kernel(*refs, final):
    if final:
        x_ref, ypg_ref, ymla_ref, mod_ref, wout_ref, gffn_ref, w1_ref, w2_ref, gfin_ref, o_ref = refs
    else:
        x_ref, ypg_ref, ymla_ref, mod_ref, wout_ref, gffn_ref, w1_ref, w2_ref, o_ref = refs
    g1 = mod_ref[0, :, 2 * D:3 * D]
    sh2 = mod_ref[0, :, 3 * D:4 * D]
    sc2 = mod_ref[0, :, 4 * D:5 * D]
    g2 = mod_ref[0, :, 5 * D:6 * D]
    ypg = ypg_ref[0]
    ycat = jnp.concatenate([ypg[:, 0:POOL_W], ymla_ref[0], ypg[:, POOL_W:]], axis=1)
    x1 = x_ref[0] + g1 * _dot(ycat, wout_ref[...])
    h = (_rms(x1, gffn_ref[...]) * (1.0 + sc2) + sh2).astype(BF16)
    acc = None
    for j in range(D_FF // FF_CHUNK):
        a = _dot(h, w1_ref[:, j * FF_CHUNK:(j + 1) * FF_CHUNK])
        a = jnp.square(jnp.maximum(a, 0.0)).astype(BF16)
        part = _dot(a, w2_ref[j * FF_CHUNK:(j + 1) * FF_CHUNK, :])
        acc = part if acc is None else acc + part
    out = x1 + g2 * acc
    if final:
        out = _rms(out, gfin_ref[...])
    o_ref[0] = out


def _post(x, ypg, ymla, mod, mod_off, lw, g_final, *, tm):
    b, seq, _ = x.shape
    final = g_final is not None

    def const(shape, single=False):
        if single:
            return pl.BlockSpec(shape, lambda bi, i: (0,) * len(shape), pipeline_mode=pl.Buffered(1))
        return pl.BlockSpec(shape, lambda bi, i: (0,) * len(shape))

    in_specs = [
        pl.BlockSpec((1, tm, D), lambda bi, i: (bi, i, 0)),
        pl.BlockSpec((1, tm, 2 * GMLP_W), lambda bi, i: (bi, i, 0)),
        pl.BlockSpec((1, tm, HEADS * V_DIM), lambda bi, i: (bi, i, 0)),
        pl.BlockSpec((1, 1, 6 * D), lambda bi, i: (bi * mod_off[1] + mod_off[0], 0, 0)),
        const((D, D), True), const((1, D)), const((D, D_FF), True), const((D_FF, D), True),
    ]
    args = [x, ypg, ymla, mod, lw["w_out"], lw["g_ffn"], lw["w_ff1"], lw["w_ff2"]]
    if final:
        in_specs.append(const((1, D)))
        args.append(g_final)
    return pl.pallas_call(
        functools.partial(_post_kernel, final=final),
        grid=(b, seq // tm),
        in_specs=in_specs,
        out_specs=pl.BlockSpec((1, tm, D), lambda bi, i: (bi, i, 0)),
        out_shape=jax.ShapeDtypeStruct((b, seq, D), F32),
        compiler_params=_params(56),
        name="post",
    )(*args)


def _rope_tables(seq):
    rows = seq // GRID_W
    row = jnp.repeat(jnp.arange(rows, dtype=F32), GRID_W)
    col = jnp.tile(jnp.arange(GRID_W, dtype=F32), rows)
    n_freq = QK_ROPE // 4
    inv = 1.0 / (ROPE_THETA ** (jnp.arange(n_freq, dtype=F32) / n_freq))
    ang = jnp.concatenate([row[:, None] * inv, col[:, None] * inv], axis=-1)
    cos, sin = jnp.cos(ang), jnp.sin(ang)
    z = jnp.zeros_like(cos)
    c = jnp.concatenate([cos, cos, z, z], axis=-1)
    s1 = jnp.concatenate([-sin, z, z, z], axis=-1)
    s2 = jnp.concatenate([z, sin, z, z], axis=-1)
    return c, s1, s2


def _layer_weights(l, w):
    return {k: v[l] for k, v in w.items()}


def kernel(x_prompt, x_sample, cache_ckv, cache_krope, c, c_ctx, w_ada, b_ada, g_mix, w_in, w_pool,
           pool_scale, g_q, w_uq, g_kv, w_ukv, g_sgu, w_s, b_s, w_out, g_ffn, w_ff1, w_ff2, g_final):
    dec_b = x_sample.shape[0]
    w_in_p = jnp.concatenate(
        [w_in[:, :, :P_R + QK_ROPE], jnp.zeros((DEPTH, D, 128 - QK_ROPE), F32), w_in[:, :, P_R + QK_ROPE:]],
        axis=-1).astype(BF16)
    w_uq_h = w_uq.reshape(DEPTH, Q_RANK, HEADS, QK_NOPE + QK_ROPE)
    w_uq_p = jnp.pad(w_uq_h, ((0, 0), (0, 0), (0, 0), (0, QK_PAD - QK_NOPE - QK_ROPE)))
    w_uq_p = w_uq_p.reshape(DEPTH, Q_RANK, HEADS * QK_PAD).astype(BF16)
    w_pool_bd = jnp.zeros((DEPTH, POOL_W, POOL_W), F32)
    for g in range(len(POOL_WINDOWS)):
        sl = slice(g * POOL_GD, (g + 1) * POOL_GD)
        w_pool_bd = w_pool_bd.at[:, sl, sl].set(w_pool[:, g])
    weights = {
        "g_mix": g_mix.reshape(DEPTH, 1, D),
        "w_in": w_in_p,
        "g_q": g_q.reshape(DEPTH, 1, Q_RANK),
        "w_uq": w_uq_p,
        "g_kv": g_kv.reshape(DEPTH, 1, KV_RANK),
        "w_ukv": w_ukv.astype(BF16),
        "w_pool": w_pool_bd.astype(BF16),
        "pool_scale": pool_scale.reshape(DEPTH, 1, POOL_W),
        "g_sgu": g_sgu.reshape(DEPTH, 1, GMLP_W),
        "w_s": w_s.reshape(DEPTH, GMLP_G * CHUNK, CHUNK).astype(BF16),
        "b_s": jnp.repeat(jnp.swapaxes(b_s, 1, 2), GMLP_GD, axis=2),
        "w_out": w_out.astype(BF16),
        "g_ffn": g_ffn.reshape(DEPTH, 1, D),
        "w_ff1": w_ff1.astype(BF16),
        "w_ff2": w_ff2.astype(BF16),
    }
    g_fin = g_final.reshape(1, D)
    rope_tabs = _rope_tables(x_sample.shape[1])

    n_rows = 16
    c_all = jnp.zeros((n_rows, D), F32).at[0].set(c_ctx).at[1:1 + dec_b].set(c)
    mod = _ada(c_all, w_ada, b_ada).reshape(DEPTH, n_rows, 1, 6 * D)

    cache_kr_pad = jnp.pad(cache_krope, ((0, 0), (0, 0), (0, 0), (0, 128 - QK_ROPE))).astype(BF16)
    cache = _cache_expand(cache_ckv, cache_kr_pad, weights["w_ukv"])

    xp, xs = x_prompt, x_sample
    ckv_list, kr_list = [], []
    for l in range(DEPTH):
        lw = _layer_weights(l, weights)
        last = l == DEPTH - 1
        q, k, v, ypg, sckv, skr = _pre(xp, mod[l], (0, 0), lw, None, tm=256, write_state=True)
        ymla = _attn(q, k, v, None, l, tq=256)
        xp = _post(xp, ypg, ymla, mod[l], (0, 0), lw, g_fin if last else None, tm=256)
        ckv_list.append(sckv)
        kr_list.append(skr)
        q, k, v, ypg = _pre(xs, mod[l], (1, 1), lw, rope_tabs, tm=512, write_state=False)
        ymla = _attn(q, k, v, cache, l, tq=512)
        xs = _post(xs, ypg, ymla, mod[l], (1, 1), lw, g_fin if last else None, tm=512)
    return xp, xs, jnp.stack(ckv_list, axis=1), jnp.stack(kr_list, axis=1)
```

```python
import functools
import math

import jax
import jax.numpy as jnp
from jax.experimental import pallas as pl
from jax.experimental.pallas import tpu as pltpu

D = 1024
DEPTH = 4
GRID_W = 64
EPS = 1e-6
POOL_W = 256
POOL_WINDOWS = (2, 4, 8, 16)
POOL_GD = 64
HALO = 8
QK_NOPE = 128
QK_ROPE = 64
V_DIM = 128
HEADS = 4
Q_RANK = 384
KV_RANK = 256
ROPE_THETA = 10000.0
CHUNK = 128
GMLP_W = 256
GMLP_G = 4
GMLP_GD = 64
D_FF = 4096
FF_CHUNK = 1024
P_Q = 256
P_KV = P_Q + Q_RANK
P_R = P_KV + KV_RANK
P_G = P_R + 128
P_END = P_G + 2 * GMLP_W
QK_PAD = 256
Q_UNIT = 256
LOOKAHEAD = 2
SM_SCALE = 1.0 / math.sqrt(QK_NOPE + QK_ROPE)
NT_DIMS = (((1,), (1,)), ((), ()))

F32 = jnp.float32
BF16 = jnp.bfloat16
MIB = 1024 * 1024


def _params(vmem_mib):
    return pltpu.CompilerParams(
        dimension_semantics=("arbitrary", "arbitrary"),
        vmem_limit_bytes=vmem_mib * MIB,
    )


def _rms(x, g):
    y = x * jax.lax.rsqrt(jnp.mean(x * x, axis=-1, keepdims=True) + EPS)
    return y * g


def _dot(a, b):
    return jnp.dot(a, b, preferred_element_type=F32)


def _dot_nt(a, b):
    return jax.lax.dot_general(a, b, NT_DIMS, preferred_element_type=F32)


def _layer_spec(shape, layer):
    return pl.BlockSpec((None,) + shape, lambda bi, i: (layer,) + (0,) * len(shape))


def _ada_kernel(c_ref, w_ref, b_ref, o_ref):
    c = c_ref[...]
    s = c / (1.0 + jnp.exp(-c))
    o_ref[0] = _dot(s.astype(BF16), w_ref[0].astype(BF16)) + b_ref[0]


def _ada(c_all, w_ada, b_ada):
    rows = c_all.shape[0]
    tn = 1536
    return pl.pallas_call(
        _ada_kernel,
        grid=(DEPTH, 6 * D // tn),
        in_specs=[
            pl.BlockSpec((rows, D), lambda l, j: (0, 0)),
            pl.BlockSpec((1, D, tn), lambda l, j: (l, 0, j)),
            pl.BlockSpec((1, 1, tn), lambda l, j: (l, 0, j)),
        ],
        out_specs=pl.BlockSpec((1, rows, tn), lambda l, j: (l, 0, j)),
        out_shape=jax.ShapeDtypeStruct((DEPTH, rows, 6 * D), F32),
        compiler_params=_params(32),
        name="ada",
    )(c_all, w_ada, b_ada.reshape(DEPTH, 1, 6 * D))


def _cache_kernel(ckv_ref, kr_ref, wk_ref, wvt_ref, k_ref, vt_ref):
    ckv = ckv_ref[0, 0].astype(BF16)
    kn = _dot(ckv, wk_ref[...])
    vt = _dot_nt(wvt_ref[...], ckv)
    kr = kr_ref[0, 0]
    for h in range(HEADS):
        k_ref[0, 0, h] = jnp.concatenate([kn[:, h * QK_NOPE:(h + 1) * QK_NOPE].astype(BF16), kr], axis=1)
        vt_ref[0, 0, h] = vt[h * V_DIM:(h + 1) * V_DIM].astype(BF16)


def _cache_expand(cache_ckv, cache_kr_pad, w_uk, w_uvt):
    b, depth, p, _ = cache_ckv.shape
    return pl.pallas_call(
        _cache_kernel,
        grid=(b, depth),
        in_specs=[
            pl.BlockSpec((1, 1, p, KV_RANK), lambda i, l: (i, l, 0, 0)),
            pl.BlockSpec((1, 1, p, 128), lambda i, l: (i, l, 0, 0)),
            pl.BlockSpec((None, KV_RANK, HEADS * QK_NOPE), lambda i, l: (l, 0, 0)),
            pl.BlockSpec((None, HEADS * V_DIM, KV_RANK), lambda i, l: (l, 0, 0)),
        ],
        out_specs=[
            pl.BlockSpec((1, 1, HEADS, p, QK_PAD), lambda i, l: (i, l, 0, 0, 0)),
            pl.BlockSpec((1, 1, HEADS, V_DIM, p), lambda i, l: (i, l, 0, 0, 0)),
        ],
        out_shape=[
            jax.ShapeDtypeStruct((b, depth, HEADS, p, QK_PAD), BF16),
            jax.ShapeDtypeStruct((b, depth, HEADS, V_DIM, p), BF16),
        ],
        compiler_params=_params(32),
        name="cache_expand",
    )(cache_ckv, cache_kr_pad, w_uk, w_uvt)


def _shift_up(a, k):
    return pltpu.roll(a, a.shape[0] - k, axis=0)


def _rope(x, c, s1, s2):
    return x * c + pltpu.roll(x, 96, axis=1) * s1 + pltpu.roll(x, 32, axis=1) * s2


def _pre_kernel(*refs, tm, seq, use_rope, write_state):
    it = iter(refs)
    x_ref, xp_ref, xn_ref, mod_ref = next(it), next(it), next(it), next(it)
    gmix_ref, win_ref = next(it), next(it)
    gq_ref, wuq_ref, gkv_ref, wuk_ref, wuvt_ref = next(it), next(it), next(it), next(it), next(it)
    wpool_ref, pscale_ref = next(it), next(it)
    gsgu_ref, ws_ref, bs_ref = next(it), next(it), next(it)
    if use_rope:
        rc_ref, rs1_ref, rs2_ref = next(it), next(it), next(it)
    q_ref, k_ref, vt_ref, ypg_ref = next(it), next(it), next(it), next(it)
    if write_state:
        sckv_ref, skr_ref = next(it), next(it)

    i = pl.program_id(1)
    rows = tm + 2 * HALO
    xe = jnp.concatenate([xp_ref[0], x_ref[0], xn_ref[0]], axis=0)
    sh1 = mod_ref[:, 0:D]
    sc1 = mod_ref[:, D:2 * D]
    h = _rms(xe, gmix_ref[...]) * (1.0 + sc1) + sh1
    proj_e = _dot(h.astype(BF16), win_ref[...])
    proj = proj_e[HALO:HALO + tm]

    pos = i * tm - HALO + jax.lax.broadcasted_iota(jnp.int32, (rows, 1), 0)
    hp_e = jnp.where((pos >= 0) & (pos < seq), proj_e[:, 0:POOL_W], 0.0)
    a2 = hp_e + _shift_up(hp_e, 1)
    a4 = a2 + _shift_up(a2, 2)
    a8 = a4 + _shift_up(a4, 4)
    a16 = a8 + _shift_up(a8, 8)
    win2 = _shift_up(a2, HALO - 1)[:tm]
    win4 = _shift_up(a4, HALO - 2)[:tm]
    win8 = _shift_up(a8, HALO - 4)[:tm]
    win16 = a16[:tm]
    grp = jax.lax.broadcasted_iota(jnp.int32, (1, POOL_W), 1) // POOL_GD
    win = jnp.where(grp == 0, win2, jnp.where(grp == 1, win4, jnp.where(grp == 2, win8, win16)))
    half = jnp.where(grp == 0, 1, jnp.where(grp == 1, 2, jnp.where(grp == 2, 4, 8)))
    t = i * tm + jax.lax.broadcasted_iota(jnp.int32, (tm, 1), 0)
    cnt = jnp.clip(t + half, 0, seq) - jnp.clip(t - half, 0, seq)
    pooled = win / cnt.astype(F32) - proj[:, 0:POOL_W]
    y_pool = _dot(pooled.astype(BF16), wpool_ref[...]) * pscale_ref[...]

    cq = _rms(proj[:, P_Q:P_KV], gq_ref[...])
    q = _dot(cq.astype(BF16), wuq_ref[...])
    ckv = _rms(proj[:, P_KV:P_R], gkv_ref[...])
    ckv_b = ckv.astype(BF16)
    kn = _dot(ckv_b, wuk_ref[...])
    vt = _dot_nt(wuvt_ref[...], ckv_b)
    kr = proj[:, P_R:P_G]
    if write_state:
        sckv_ref[0] = ckv
        skr_ref[0] = kr[:, 0:QK_ROPE]
    if use_rope:
        rc, rs1, rs2 = rc_ref[...], rs1_ref[...], rs2_ref[...]
        kr = _rope(kr, rc, rs1, rs2)
    kr_b = kr.astype(BF16)
    for hd in range(HEADS):
        qn = q[:, hd * QK_PAD:hd * QK_PAD + QK_NOPE]
        qr = q[:, hd * QK_PAD + QK_NOPE:(hd + 1) * QK_PAD]
        if use_rope:
            qr = _rope(qr, rc, rs1, rs2)
        q_ref[0, hd] = (jnp.concatenate([qn, qr], axis=1) * SM_SCALE).astype(BF16)
        k_ref[0, hd] = jnp.concatenate([kn[:, hd * QK_NOPE:(hd + 1) * QK_NOPE].astype(BF16), kr_b], axis=1)
        vt_ref[0, hd] = vt[hd * V_DIM:(hd + 1) * V_DIM].astype(BF16)

    uv = jax.nn.gelu(proj[:, P_G:P_END])
    u = uv[:, 0:GMLP_W]
    vg = _rms(uv[:, GMLP_W:], gsgu_ref[...]).astype(BF16)
    ggrp = jax.lax.broadcasted_iota(jnp.int32, (1, GMLP_W), 1) // GMLP_GD
    ys = []
    for cix in range(tm // CHUNK):
        r = _dot(ws_ref[...], vg[cix * CHUNK:(cix + 1) * CHUNK])
        mixed = jnp.where(
            ggrp == 0, r[0:CHUNK],
            jnp.where(ggrp == 1, r[CHUNK:2 * CHUNK],
                      jnp.where(ggrp == 2, r[2 * CHUNK:3 * CHUNK], r[3 * CHUNK:4 * CHUNK])))
        ys.append(u[cix * CHUNK:(cix + 1) * CHUNK] * (mixed + bs_ref[...]))
    y_g = jnp.concatenate(ys, axis=0)
    ypg_ref[0] = jnp.concatenate([y_pool, y_g], axis=1).astype(BF16)


def _pre(x, mod, mod_off, layer, w, rope_tabs, *, tm, write_state):
    b, seq, _ = x.shape
    nt = seq // tm
    use_rope = rope_tabs is not None
    hb = tm // HALO
    nhb = seq // HALO
    ls = functools.partial(_layer_spec, layer=layer)

    in_specs = [
        pl.BlockSpec((1, tm, D), lambda bi, i: (bi, i, 0)),
        pl.BlockSpec((1, HALO, D), lambda bi, i: (bi, jnp.maximum(i * hb - 1, 0), 0)),
        pl.BlockSpec((1, HALO, D), lambda bi, i: (bi, jnp.minimum((i + 1) * hb, nhb - 1), 0)),
        pl.BlockSpec((None, None, 1, 6 * D), lambda bi, i: (layer, bi * mod_off[1] + mod_off[0], 0, 0)),
        ls((1, D)), ls((D, P_END)),
        ls((1, Q_RANK)), ls((Q_RANK, HEADS * QK_PAD)),
        ls((1, KV_RANK)), ls((KV_RANK, HEADS * QK_NOPE)), ls((HEADS * V_DIM, KV_RANK)),
        ls((POOL_W, POOL_W)), ls((1, POOL_W)),
        ls((1, GMLP_W)), ls((GMLP_G * CHUNK, CHUNK)), ls((CHUNK, GMLP_W)),
    ]
    args = [x, x, x, mod, w["g_mix"], w["w_in"], w["g_q"], w["w_uq"], w["g_kv"], w["w_uk"], w["w_uvt"],
            w["w_pool"], w["pool_scale"], w["g_sgu"], w["w_s"], w["b_s"]]
    if use_rope:
        in_specs += [pl.BlockSpec((tm, 128), lambda bi, i: (i, 0))] * 3
        args += list(rope_tabs)
    out_specs = [
        pl.BlockSpec((1, HEADS, tm, QK_PAD), lambda bi, i: (bi, 0, i, 0)),
        pl.BlockSpec((1, HEADS, tm, QK_PAD), lambda bi, i: (bi, 0, i, 0)),
        pl.BlockSpec((1, HEADS, V_DIM, tm), lambda bi, i: (bi, 0, 0, i)),
        pl.BlockSpec((1, tm, 2 * GMLP_W), lambda bi, i: (bi, i, 0)),
    ]
    out_shape = [
        jax.ShapeDtypeStruct((b, HEADS, seq, QK_PAD), BF16),
        jax.ShapeDtypeStruct((b, HEADS, seq, QK_PAD), BF16),
        jax.ShapeDtypeStruct((b, HEADS, V_DIM, seq), BF16),
        jax.ShapeDtypeStruct((b, seq, 2 * GMLP_W), BF16),
    ]
    if write_state:
        out_specs += [
            pl.BlockSpec((1, tm, KV_RANK), lambda bi, i: (bi, i, 0)),
            pl.BlockSpec((1, tm, QK_ROPE), lambda bi, i: (bi, i, 0)),
        ]
        out_shape += [
            jax.ShapeDtypeStruct((b, seq, KV_RANK), F32),
            jax.ShapeDtypeStruct((b, seq, QK_ROPE), F32),
        ]
    return pl.pallas_call(
        functools.partial(_pre_kernel, tm=tm, seq=seq, use_rope=use_rope, write_state=write_state),
        grid=(b, nt),
        in_specs=in_specs,
        out_specs=out_specs,
        out_shape=out_shape,
        compiler_params=_params(48),
        name="pre_lat" if use_rope else "pre_ctx",
    )(*args)


def _attn_kernel(*refs, tq, with_cache):
    if with_cache:
        q_ref, k_ref, vt_ref, kc_ref, vct_ref, o_ref = refs
    else:
        q_ref, k_ref, vt_ref, o_ref = refs
    units = [(h, slice(u * Q_UNIT, (u + 1) * Q_UNIT)) for h in range(HEADS) for u in range(tq // Q_UNIT)]

    def scores(unit):
        h, rows = unit
        q = q_ref[0, h, rows, :]
        st = _dot_nt(k_ref[0, h], q)
        sct = _dot_nt(kc_ref[0, 0, h], q) if with_cache else None
        return st, sct

    ahead = [scores(u) for u in units[:LOOKAHEAD]]
    for idx, (h, rows) in enumerate(units):
        st, sct = ahead.pop(0)
        if idx + LOOKAHEAD < len(units):
            ahead.append(scores(units[idx + LOOKAHEAD]))
        m = jnp.max(st, axis=0, keepdims=True)
        if with_cache:
            m = jnp.maximum(m, jnp.max(sct, axis=0, keepdims=True))
            pct = jnp.exp(sct - m)
        pt = jnp.exp(st - m)
        den = jnp.sum(pt, axis=0, keepdims=True)
        ot = _dot(vt_ref[0, h], pt.astype(BF16))
        if with_cache:
            den = den + jnp.sum(pct, axis=0, keepdims=True)
            ot = ot + _dot(vct_ref[0, 0, h], pct.astype(BF16))
        o_ref[0, rows, h * V_DIM:(h + 1) * V_DIM] = (ot / den).T.astype(BF16)


def _attn(q, k, vt, cache, layer, *, tq):
    b, _, seq, _ = q.shape
    with_cache = cache is not None
    in_specs = [
        pl.BlockSpec((1, HEADS, tq, QK_PAD), lambda bi, i: (bi, 0, i, 0)),
        pl.BlockSpec((1, HEADS, seq, QK_PAD), lambda bi, i: (bi, 0, 0, 0)),
        pl.BlockSpec((1, HEADS, V_DIM, seq), lambda bi, i: (bi, 0, 0, 0)),
    ]
    args = [q, k, vt]
    if with_cache:
        kc, vct = cache
        p = kc.shape[3]
        in_specs += [
            pl.BlockSpec((1, 1, HEADS, p, QK_PAD), lambda bi, i: (bi, layer, 0, 0, 0)),
            pl.BlockSpec((1, 1, HEADS, V_DIM, p), lambda bi, i: (bi, layer, 0, 0, 0)),
        ]
        args += [kc, vct]
    return pl.pallas_call(
        functools.partial(_attn_kernel, tq=tq, with_cache=with_cache),
        grid=(b, seq // tq),
        in_specs=in_specs,
        out_specs=pl.BlockSpec((1, tq, HEADS * V_DIM), lambda bi, i: (bi, i, 0)),
        out_shape=jax.ShapeDtypeStruct((b, seq, HEADS * V_DIM), BF16),
        compiler_params=_params(48),
        name="attn_lat" if with_cache else "attn_ctx",
    )(*args)


def _post_kernel(*refs, final):
    if final:
        x_ref, ypg_ref, ymla_ref, mod_ref, wout_ref, gffn_ref, w1_ref, w2_ref, gfin_ref, o_ref = refs
    else:
        x_ref, ypg_ref, ymla_ref, mod_ref, wout_ref, gffn_ref, w1_ref, w2_ref, o_ref = refs
    g1 = mod_ref[:, 2 * D:3 * D]
    sh2 = mod_ref[:, 3 * D:4 * D]
    sc2 = mod_ref[:, 4 * D:5 * D]
    g2 = mod_ref[:, 5 * D:6 * D]
    ypg = ypg_ref[0]
    ycat = jnp.concatenate([ypg[:, 0:POOL_W], ymla_ref[0], ypg[:, POOL_W:]], axis=1)
    x1 = x_ref[0] + g1 * _dot(ycat, wout_ref[...])
    h = (_rms(x1, gffn_ref[...]) * (1.0 + sc2) + sh2).astype(BF16)
    acc = None
    for j in range(D_FF // FF_CHUNK):
        a = _dot(h, w1_ref[:, j * FF_CHUNK:(j + 1) * FF_CHUNK])
        a = jnp.square(jnp.maximum(a, 0.0)).astype(BF16)
        part = _dot(a, w2_ref[j * FF_CHUNK:(j + 1) * FF_CHUNK, :])
        acc = part if acc is None else acc + part
    out = x1 + g2 * acc
    if final:
        out = _rms(out, gfin_ref[...])
    o_ref[0] = out


def _post(x, ypg, ymla, mod, mod_off, layer, w, g_final, *, tm):
    b, seq, _ = x.shape
    final = g_final is not None

    def big(shape):
        return pl.BlockSpec((None,) + shape, lambda bi, i: (layer,) + (0,) * len(shape),
                            pipeline_mode=pl.Buffered(1))

    in_specs = [
        pl.BlockSpec((1, tm, D), lambda bi, i: (bi, i, 0)),
        pl.BlockSpec((1, tm, 2 * GMLP_W), lambda bi, i: (bi, i, 0)),
        pl.BlockSpec((1, tm, HEADS * V_DIM), lambda bi, i: (bi, i, 0)),
        pl.BlockSpec((None, None, 1, 6 * D), lambda bi, i: (layer, bi * mod_off[1] + mod_off[0], 0, 0)),
        big((D, D)), _layer_spec((1, D), layer), big((D, D_FF)), big((D_FF, D)),
    ]
    args = [x, ypg, ymla, mod, w["w_out"], w["g_ffn"], w["w_ff1"], w["w_ff2"]]
    if final:
        in_specs.append(pl.BlockSpec((1, D), lambda bi, i: (0, 0)))
        args.append(g_final)
    return pl.pallas_call(
        functools.partial(_post_kernel, final=final),
        grid=(b, seq // tm),
        in_specs=in_specs,
        out_specs=pl.BlockSpec((1, tm, D), lambda bi, i: (bi, i, 0)),
        out_shape=jax.ShapeDtypeStruct((b, seq, D), F32),
        compiler_params=_params(56),
        name="post",
    )(*args)


def _rope_tables(seq):
    rows = seq // GRID_W
    row = jnp.repeat(jnp.arange(rows, dtype=F32), GRID_W)
    col = jnp.tile(jnp.arange(GRID_W, dtype=F32), rows)
    n_freq = QK_ROPE // 4
    inv = 1.0 / (ROPE_THETA ** (jnp.arange(n_freq, dtype=F32) / n_freq))
    ang = jnp.concatenate([row[:, None] * inv, col[:, None] * inv], axis=-1)
    cos, sin = jnp.cos(ang), jnp.sin(ang)
    z = jnp.zeros_like(cos)
    c = jnp.concatenate([cos, cos, z, z], axis=-1)
    s1 = jnp.concatenate([-sin, z, z, z], axis=-1)
    s2 = jnp.concatenate([z, sin, z, z], axis=-1)
    return c, s1, s2


def kernel(x_prompt, x_sample, cache_ckv, cache_krope, c, c_ctx, w_ada, b_ada, g_mix, w_in, w_pool,
           pool_scale, g_q, w_uq, g_kv, w_ukv, g_sgu, w_s, b_s, w_out, g_ffn, w_ff1, w_ff2, g_final):
    dec_b = x_sample.shape[0]
    w_in_b = w_in.astype(BF16)
    w_in_p = jnp.concatenate(
        [w_in_b[:, :, :P_R + QK_ROPE], jnp.zeros((DEPTH, D, 128 - QK_ROPE), BF16), w_in_b[:, :, P_R + QK_ROPE:]],
        axis=-1)
    w_uq_h = w_uq.astype(BF16).reshape(DEPTH, Q_RANK, HEADS, QK_NOPE + QK_ROPE)
    w_uq_p = jnp.pad(w_uq_h, ((0, 0), (0, 0), (0, 0), (0, QK_PAD - QK_NOPE - QK_ROPE)))
    w_uq_p = w_uq_p.reshape(DEPTH, Q_RANK, HEADS * QK_PAD)
    w_ukv_h = w_ukv.astype(BF16).reshape(DEPTH, KV_RANK, HEADS, QK_NOPE + V_DIM)
    w_uk = w_ukv_h[..., :QK_NOPE].reshape(DEPTH, KV_RANK, HEADS * QK_NOPE)
    w_uvt = jnp.swapaxes(w_ukv_h[..., QK_NOPE:].reshape(DEPTH, KV_RANK, HEADS * V_DIM), 1, 2)
    w_pool_bd = jnp.zeros((DEPTH, POOL_W, POOL_W), BF16)
    for g in range(len(POOL_WINDOWS)):
        sl = slice(g * POOL_GD, (g + 1) * POOL_GD)
        w_pool_bd = w_pool_bd.at[:, sl, sl].set(w_pool[:, g].astype(BF16))
    weights = {
        "g_mix": g_mix.reshape(DEPTH, 1, D),
        "w_in": w_in_p,
        "g_q": g_q.reshape(DEPTH, 1, Q_RANK),
        "w_uq": w_uq_p,
        "g_kv": g_kv.reshape(DEPTH, 1, KV_RANK),
        "w_uk": w_uk,
        "w_uvt": w_uvt,
        "w_pool": w_pool_bd,
        "pool_scale": pool_scale.reshape(DEPTH, 1, POOL_W),
        "g_sgu": g_sgu.reshape(DEPTH, 1, GMLP_W),
        "w_s": w_s.reshape(DEPTH, GMLP_G * CHUNK, CHUNK).astype(BF16),
        "b_s": jnp.repeat(jnp.swapaxes(b_s, 1, 2), GMLP_GD, axis=2),
        "w_out": w_out.astype(BF16),
        "g_ffn": g_ffn.reshape(DEPTH, 1, D),
        "w_ff1": w_ff1.astype(BF16),
        "w_ff2": w_ff2.astype(BF16),
    }
    g_fin = g_final.reshape(1, D)
    rope_tabs = _rope_tables(x_sample.shape[1])

    n_rows = 16
    c_all = jnp.zeros((n_rows, D), F32).at[0].set(c_ctx).at[1:1 + dec_b].set(c)
    mod = _ada(c_all, w_ada, b_ada).reshape(DEPTH, n_rows, 1, 6 * D)

    cache_kr_pad = jnp.pad(cache_krope, ((0, 0), (0, 0), (0, 0), (0, 128 - QK_ROPE))).astype(BF16)
    cache = _cache_expand(cache_ckv, cache_kr_pad, w_uk, w_uvt)

    xp, xs = x_prompt, x_sample
    ckv_list, kr_list = [], []
    for l in range(DEPTH):
        last = l == DEPTH - 1
        q, k, vt, ypg, sckv, skr = _pre(xp, mod, (0, 0), l, weights, None, tm=256, write_state=True)
        ymla = _attn(q, k, vt, None, l, tq=256)
        xp = _post(xp, ypg, ymla, mod, (0, 0), l, weights, g_fin if last else None, tm=256)
        ckv_list.append(sckv)
        kr_list.append(skr)
        q, k, vt, ypg = _pre(xs, mod, (1, 1), l, weights, rope_tabs, tm=512, write_state=False)
        ymla = _attn(q, k, vt, cache, l, tq=512)
        xs = _post(xs, ypg, ymla, mod, (1, 1), l, weights, g_fin if last else None, tm=512)
    return xp, xs, jnp.stack(ckv_list, axis=1), jnp.stack(kr_list, axis=1)
```

```python
import functools
import math

import jax
import jax.numpy as jnp
from jax.experimental import pallas as pl
from jax.experimental.pallas import tpu as pltpu

D = 1024
DEPTH = 4
GRID_W = 64
EPS = 1e-6
POOL_W = 256
POOL_WINDOWS = (2, 4, 8, 16)
POOL_GD = 64
HALO = 8
QK_NOPE = 128
QK_ROPE = 64
V_DIM = 128
V_AUG = V_DIM + 16
HEADS = 4
Q_RANK = 384
KV_RANK = 256
ROPE_THETA = 10000.0
CHUNK = 128
GMLP_W = 256
GMLP_G = 4
GMLP_GD = 64
D_FF = 4096
FF_CHUNK = 1024
P_Q = 256
P_KV = P_Q + Q_RANK
P_R = P_KV + KV_RANK
P_G = P_R + 128
P_END = P_G + 2 * GMLP_W
QK_PAD = 256
Q_UNIT = 256
LOOKAHEAD = 2
SM_SCALE = math.log2(math.e) / math.sqrt(QK_NOPE + QK_ROPE)
NT_DIMS = (((1,), (1,)), ((), ()))

F32 = jnp.float32
BF16 = jnp.bfloat16
MIB = 1024 * 1024


def _params(vmem_mib):
    return pltpu.CompilerParams(
        dimension_semantics=("arbitrary", "arbitrary"),
        vmem_limit_bytes=vmem_mib * MIB,
    )


def _rms(x, g):
    y = x * jax.lax.rsqrt(jnp.mean(x * x, axis=-1, keepdims=True) + EPS)
    return y * g


def _dot(a, b):
    return jnp.dot(a, b, preferred_element_type=F32)


def _dot_nt(a, b):
    return jax.lax.dot_general(a, b, NT_DIMS, preferred_element_type=F32)


def _layer_spec(shape, layer):
    return pl.BlockSpec((None,) + shape, lambda bi, i: (layer,) + (0,) * len(shape))


def _ada_kernel(c_ref, w_ref, b_ref, o_ref):
    c = c_ref[...]
    s = c / (1.0 + jnp.exp(-c))
    o_ref[0] = _dot(s.astype(BF16), w_ref[0].astype(BF16)) + b_ref[0]


def _ada(c_all, w_ada, b_ada):
    rows = c_all.shape[0]
    tn = 1536
    return pl.pallas_call(
        _ada_kernel,
        grid=(DEPTH, 6 * D // tn),
        in_specs=[
            pl.BlockSpec((rows, D), lambda l, j: (0, 0)),
            pl.BlockSpec((1, D, tn), lambda l, j: (l, 0, j)),
            pl.BlockSpec((1, 1, tn), lambda l, j: (l, 0, j)),
        ],
        out_specs=pl.BlockSpec((1, rows, tn), lambda l, j: (l, 0, j)),
        out_shape=jax.ShapeDtypeStruct((DEPTH, rows, 6 * D), F32),
        compiler_params=_params(32),
        name="ada",
    )(c_all, w_ada, b_ada.reshape(DEPTH, 1, 6 * D))


def _cache_kernel(ckv_ref, kr_ref, wk_ref, wvt_ref, k_ref, vt_ref):
    ckv = ckv_ref[0, 0].astype(BF16)
    kn = _dot(ckv, wk_ref[...])
    vt = _dot_nt(wvt_ref[...], ckv)
    kr = kr_ref[0, 0]
    for h in range(HEADS):
        k_ref[0, 0, h] = jnp.concatenate([kn[:, h * QK_NOPE:(h + 1) * QK_NOPE].astype(BF16), kr], axis=1)
        vt_ref[0, 0, h, 0:V_DIM] = vt[h * V_DIM:(h + 1) * V_DIM].astype(BF16)
        vt_ref[0, 0, h, V_DIM:V_AUG] = jnp.ones((V_AUG - V_DIM, vt.shape[1]), BF16)


def _cache_expand(cache_ckv, cache_kr_pad, w_uk, w_uvt):
    b, depth, p, _ = cache_ckv.shape
    return pl.pallas_call(
        _cache_kernel,
        grid=(b, depth),
        in_specs=[
            pl.BlockSpec((1, 1, p, KV_RANK), lambda i, l: (i, l, 0, 0)),
            pl.BlockSpec((1, 1, p, 128), lambda i, l: (i, l, 0, 0)),
            pl.BlockSpec((None, KV_RANK, HEADS * QK_NOPE), lambda i, l: (l, 0, 0)),
            pl.BlockSpec((None, HEADS * V_DIM, KV_RANK), lambda i, l: (l, 0, 0)),
        ],
        out_specs=[
            pl.BlockSpec((1, 1, HEADS, p, QK_PAD), lambda i, l: (i, l, 0, 0, 0)),
            pl.BlockSpec((1, 1, HEADS, V_AUG, p), lambda i, l: (i, l, 0, 0, 0)),
        ],
        out_shape=[
            jax.ShapeDtypeStruct((b, depth, HEADS, p, QK_PAD), BF16),
            jax.ShapeDtypeStruct((b, depth, HEADS, V_AUG, p), BF16),
        ],
        compiler_params=_params(32),
        name="cache_expand",
    )(cache_ckv, cache_kr_pad, w_uk, w_uvt)


def _shift_up(a, k):
    return pltpu.roll(a, a.shape[0] - k, axis=0)


def _rope(x, c, s1, s2):
    return x * c + pltpu.roll(x, 96, axis=1) * s1 + pltpu.roll(x, 32, axis=1) * s2


def _pre_kernel(*refs, tm, seq, use_rope, write_state):
    it = iter(refs)
    x_ref, xp_ref, xn_ref, mod_ref = next(it), next(it), next(it), next(it)
    gmix_ref, win_ref = next(it), next(it)
    gq_ref, wuq_ref, gkv_ref, wuk_ref, wuvt_ref = next(it), next(it), next(it), next(it), next(it)
    wpool_ref, pscale_ref = next(it), next(it)
    gsgu_ref, ws_ref, bs_ref = next(it), next(it), next(it)
    if use_rope:
        rc_ref, rs1_ref, rs2_ref = next(it), next(it), next(it)
    q_ref, k_ref, vt_ref, ypg_ref = next(it), next(it), next(it), next(it)
    if write_state:
        sckv_ref, skr_ref = next(it), next(it)

    i = pl.program_id(1)
    rows = tm + 2 * HALO
    xe = jnp.concatenate([xp_ref[0], x_ref[0], xn_ref[0]], axis=0)
    sh1 = mod_ref[:, 0:D]
    sc1 = mod_ref[:, D:2 * D]
    h = _rms(xe, gmix_ref[...]) * (1.0 + sc1) + sh1
    proj_e = _dot(h.astype(BF16), win_ref[...])
    proj = proj_e[HALO:HALO + tm]

    pos = i * tm - HALO + jax.lax.broadcasted_iota(jnp.int32, (rows, 1), 0)
    hp_e = jnp.where((pos >= 0) & (pos < seq), proj_e[:, 0:POOL_W], 0.0)
    a2 = hp_e + _shift_up(hp_e, 1)
    a4 = a2 + _shift_up(a2, 2)
    a8 = a4 + _shift_up(a4, 4)
    a16 = a8 + _shift_up(a8, 8)
    win2 = _shift_up(a2, HALO - 1)[:tm]
    win4 = _shift_up(a4, HALO - 2)[:tm]
    win8 = _shift_up(a8, HALO - 4)[:tm]
    win16 = a16[:tm]
    grp = jax.lax.broadcasted_iota(jnp.int32, (1, POOL_W), 1) // POOL_GD
    win = jnp.where(grp == 0, win2, jnp.where(grp == 1, win4, jnp.where(grp == 2, win8, win16)))
    half = jnp.where(grp == 0, 1, jnp.where(grp == 1, 2, jnp.where(grp == 2, 4, 8)))
    t = i * tm + jax.lax.broadcasted_iota(jnp.int32, (tm, 1), 0)
    cnt = jnp.clip(t + half, 0, seq) - jnp.clip(t - half, 0, seq)
    pooled = win / cnt.astype(F32) - proj[:, 0:POOL_W]
    y_pool = _dot(pooled.astype(BF16), wpool_ref[...]) * pscale_ref[...]

    cq = _rms(proj[:, P_Q:P_KV], gq_ref[...])
    q = _dot(cq.astype(BF16), wuq_ref[...])
    ckv = _rms(proj[:, P_KV:P_R], gkv_ref[...])
    ckv_b = ckv.astype(BF16)
    kn = _dot(ckv_b, wuk_ref[...])
    vt = _dot_nt(wuvt_ref[...], ckv_b)
    kr = proj[:, P_R:P_G]
    if write_state:
        sckv_ref[0] = ckv
        skr_ref[0] = kr[:, 0:QK_ROPE]
    if use_rope:
        rc, rs1, rs2 = rc_ref[...], rs1_ref[...], rs2_ref[...]
        kr = _rope(kr, rc, rs1, rs2)
    kr_b = kr.astype(BF16)
    for hd in range(HEADS):
        qn = q[:, hd * QK_PAD:hd * QK_PAD + QK_NOPE]
        qr = q[:, hd * QK_PAD + QK_NOPE:(hd + 1) * QK_PAD]
        if use_rope:
            qr = _rope(qr, rc, rs1, rs2)
        q_ref[0, hd] = (jnp.concatenate([qn, qr], axis=1) * SM_SCALE).astype(BF16)
        k_ref[0, hd] = jnp.concatenate([kn[:, hd * QK_NOPE:(hd + 1) * QK_NOPE].astype(BF16), kr_b], axis=1)
        vt_ref[0, hd, 0:V_DIM] = vt[hd * V_DIM:(hd + 1) * V_DIM].astype(BF16)
        vt_ref[0, hd, V_DIM:V_AUG] = jnp.ones((V_AUG - V_DIM, tm), BF16)

    uv = jax.nn.gelu(proj[:, P_G:P_END])
    u = uv[:, 0:GMLP_W]
    vg = _rms(uv[:, GMLP_W:], gsgu_ref[...]).astype(BF16)
    ggrp = jax.lax.broadcasted_iota(jnp.int32, (1, GMLP_W), 1) // GMLP_GD
    ys = []
    for cix in range(tm // CHUNK):
        r = _dot(ws_ref[...], vg[cix * CHUNK:(cix + 1) * CHUNK])
        mixed = jnp.where(
            ggrp == 0, r[0:CHUNK],
            jnp.where(ggrp == 1, r[CHUNK:2 * CHUNK],
                      jnp.where(ggrp == 2, r[2 * CHUNK:3 * CHUNK], r[3 * CHUNK:4 * CHUNK])))
        ys.append(u[cix * CHUNK:(cix + 1) * CHUNK] * (mixed + bs_ref[...]))
    y_g = jnp.concatenate(ys, axis=0)
    ypg_ref[0] = jnp.concatenate([y_pool, y_g], axis=1).astype(BF16)


def _pre(x, mod, mod_off, layer, w, rope_tabs, *, tm, write_state):
    b, seq, _ = x.shape
    nt = seq // tm
    use_rope = rope_tabs is not None
    hb = tm // HALO
    nhb = seq // HALO
    ls = functools.partial(_layer_spec, layer=layer)

    in_specs = [
        pl.BlockSpec((1, tm, D), lambda bi, i: (bi, i, 0)),
        pl.BlockSpec((1, HALO, D), lambda bi, i: (bi, jnp.maximum(i * hb - 1, 0), 0)),
        pl.BlockSpec((1, HALO, D), lambda bi, i: (bi, jnp.minimum((i + 1) * hb, nhb - 1), 0)),
        pl.BlockSpec((None, None, 1, 6 * D), lambda bi, i: (layer, bi * mod_off[1] + mod_off[0], 0, 0)),
        ls((1, D)), ls((D, P_END)),
        ls((1, Q_RANK)), ls((Q_RANK, HEADS * QK_PAD)),
        ls((1, KV_RANK)), ls((KV_RANK, HEADS * QK_NOPE)), ls((HEADS * V_DIM, KV_RANK)),
        ls((POOL_W, POOL_W)), ls((1, POOL_W)),
        ls((1, GMLP_W)), ls((GMLP_G * CHUNK, CHUNK)), ls((CHUNK, GMLP_W)),
    ]
    args = [x, x, x, mod, w["g_mix"], w["w_in"], w["g_q"], w["w_uq"], w["g_kv"], w["w_uk"], w["w_uvt"],
            w["w_pool"], w["pool_scale"], w["g_sgu"], w["w_s"], w["b_s"]]
    if use_rope:
        in_specs += [pl.BlockSpec((tm, 128), lambda bi, i: (i, 0))] * 3
        args += list(rope_tabs)
    out_specs = [
        pl.BlockSpec((1, HEADS, tm, QK_PAD), lambda bi, i: (bi, 0, i, 0)),
        pl.BlockSpec((1, HEADS, tm, QK_PAD), lambda bi, i: (bi, 0, i, 0)),
        pl.BlockSpec((1, HEADS, V_AUG, tm), lambda bi, i: (bi, 0, 0, i)),
        pl.BlockSpec((1, tm, 2 * GMLP_W), lambda bi, i: (bi, i, 0)),
    ]
    out_shape = [
        jax.ShapeDtypeStruct((b, HEADS, seq, QK_PAD), BF16),
        jax.ShapeDtypeStruct((b, HEADS, seq, QK_PAD), BF16),
        jax.ShapeDtypeStruct((b, HEADS, V_AUG, seq), BF16),
        jax.ShapeDtypeStruct((b, seq, 2 * GMLP_W), BF16),
    ]
    if write_state:
        out_specs += [
            pl.BlockSpec((1, tm, KV_RANK), lambda bi, i: (bi, i, 0)),
            pl.BlockSpec((1, tm, QK_ROPE), lambda bi, i: (bi, i, 0)),
        ]
        out_shape += [
            jax.ShapeDtypeStruct((b, seq, KV_RANK), F32),
            jax.ShapeDtypeStruct((b, seq, QK_ROPE), F32),
        ]
    return pl.pallas_call(
        functools.partial(_pre_kernel, tm=tm, seq=seq, use_rope=use_rope, write_state=write_state),
        grid=(b, nt),
        in_specs=in_specs,
        out_specs=out_specs,
        out_shape=out_shape,
        compiler_params=_params(48),
        name="pre_lat" if use_rope else "pre_ctx",
    )(*args)


def _attn_kernel(*refs, tq, with_cache):
    if with_cache:
        q_ref, k_ref, vt_ref, kc_ref, vct_ref, o_ref = refs
    else:
        q_ref, k_ref, vt_ref, o_ref = refs
    units = [(h, slice(u * Q_UNIT, (u + 1) * Q_UNIT)) for h in range(HEADS) for u in range(tq // Q_UNIT)]

    def scores(unit):
        h, rows = unit
        q = q_ref[0, h, rows, :]
        st = _dot_nt(k_ref[0, h], q)
        sct = _dot_nt(kc_ref[0, 0, h], q) if with_cache else None
        return st, sct

    ahead = [scores(u) for u in units[:LOOKAHEAD]]
    for idx, (h, rows) in enumerate(units):
        st, sct = ahead.pop(0)
        if idx + LOOKAHEAD < len(units):
            ahead.append(scores(units[idx + LOOKAHEAD]))
        m = jnp.max(st, axis=0, keepdims=True)
        if with_cache:
            m = jnp.maximum(m, jnp.max(sct, axis=0, keepdims=True))
        ot = _dot(vt_ref[0, h], jnp.exp2(st - m).astype(BF16))
        if with_cache:
            ot = ot + _dot(vct_ref[0, 0, h], jnp.exp2(sct - m).astype(BF16))
        o = ot[0:V_DIM] / ot[V_DIM:V_DIM + 1]
        o_ref[0, rows, h * V_DIM:(h + 1) * V_DIM] = o.T.astype(BF16)


def _attn(q, k, vt, cache, layer, *, tq):
    b, _, seq, _ = q.shape
    with_cache = cache is not None
    in_specs = [
        pl.BlockSpec((1, HEADS, tq, QK_PAD), lambda bi, i: (bi, 0, i, 0)),
        pl.BlockSpec((1, HEADS, seq, QK_PAD), lambda bi, i: (bi, 0, 0, 0)),
        pl.BlockSpec((1, HEADS, V_AUG, seq), lambda bi, i: (bi, 0, 0, 0)),
    ]
    args = [q, k, vt]
    if with_cache:
        kc, vct = cache
        p = kc.shape[3]
        in_specs += [
            pl.BlockSpec((1, 1, HEADS, p, QK_PAD), lambda bi, i: (bi, layer, 0, 0, 0)),
            pl.BlockSpec((1, 1, HEADS, V_AUG, p), lambda bi, i: (bi, layer, 0, 0, 0)),
        ]
        args += [kc, vct]
    return pl.pallas_call(
        functools.partial(_attn_kernel, tq=tq, with_cache=with_cache),
        grid=(b, seq // tq),
        in_specs=in_specs,
        out_specs=pl.BlockSpec((1, tq, HEADS * V_DIM), lambda bi, i: (bi, i, 0)),
        out_shape=jax.ShapeDtypeStruct((b, seq, HEADS * V_DIM), BF16),
        compiler_params=_params(48),
        name="attn_lat" if with_cache else "attn_ctx",
    )(*args)


def _post_kernel(*refs, final):
    if final:
        x_ref, ypg_ref, ymla_ref, mod_ref, wout_ref, gffn_ref, w1_ref, w2_ref, gfin_ref, o_ref = refs
    else:
        x_ref, ypg_ref, ymla_ref, mod_ref, wout_ref, gffn_ref, w1_ref, w2_ref, o_ref = refs
    g1 = mod_ref[:, 2 * D:3 * D]
    sh2 = mod_ref[:, 3 * D:4 * D]
    sc2 = mod_ref[:, 4 * D:5 * D]
    g2 = mod_ref[:, 5 * D:6 * D]
    ypg = ypg_ref[0]
    ycat = jnp.concatenate([ypg[:, 0:POOL_W], ymla_ref[0], ypg[:, POOL_W:]], axis=1)
    x1 = x_ref[0] + g1 * _dot(ycat, wout_ref[...])
    h = (_rms(x1, gffn_ref[...]) * (1.0 + sc2) + sh2).astype(BF16)
    acc = None
    for j in range(D_FF // FF_CHUNK):
        a = _dot(h, w1_ref[:, j * FF_CHUNK:(j + 1) * FF_CHUNK])
        a = jnp.square(jnp.maximum(a, 0.0)).astype(BF16)
        part = _dot(a, w2_ref[j * FF_CHUNK:(j + 1) * FF_CHUNK, :])
        acc = part if acc is None else acc + part
    out = x1 + g2 * acc
    if final:
        out = _rms(out, gfin_ref[...])
    o_ref[0] = out


def _post(x, ypg, ymla, mod, mod_off, layer, w, g_final, *, tm):
    b, seq, _ = x.shape
    final = g_final is not None

    def big(shape):
        return pl.BlockSpec((None,) + shape, lambda bi, i: (layer,) + (0,) * len(shape),
                            pipeline_mode=pl.Buffered(1))

    in_specs = [
        pl.BlockSpec((1, tm, D), lambda bi, i: (bi, i, 0)),
        pl.BlockSpec((1, tm, 2 * GMLP_W), lambda bi, i: (bi, i, 0)),
        pl.BlockSpec((1, tm, HEADS * V_DIM), lambda bi, i: (bi, i, 0)),
        pl.BlockSpec((None, None, 1, 6 * D), lambda bi, i: (layer, bi * mod_off[1] + mod_off[0], 0, 0)),
        big((D, D)), _layer_spec((1, D), layer), big((D, D_FF)), big((D_FF, D)),
    ]
    args = [x, ypg, ymla, mod, w["w_out"], w["g_ffn"], w["w_ff1"], w["w_ff2"]]
    if final:
        in_specs.append(pl.BlockSpec((1, D), lambda bi, i: (0, 0)))
        args.append(g_final)
    return pl.pallas_call(
        functools.partial(_post_kernel, final=final),
        grid=(b, seq // tm),
        in_specs=in_specs,
        out_specs=pl.BlockSpec((1, tm, D), lambda bi, i: (bi, i, 0)),
        out_shape=jax.ShapeDtypeStruct((b, seq, D), F32),
        compiler_params=_params(56),
        name="post",
    )(*args)


def _rope_tables(seq):
    rows = seq // GRID_W
    row = jnp.repeat(jnp.arange(rows, dtype=F32), GRID_W)
    col = jnp.tile(jnp.arange(GRID_W, dtype=F32), rows)
    n_freq = QK_ROPE // 4
    inv = 1.0 / (ROPE_THETA ** (jnp.arange(n_freq, dtype=F32) / n_freq))
    ang = jnp.concatenate([row[:, None] * inv, col[:, None] * inv], axis=-1)
    cos, sin = jnp.cos(ang), jnp.sin(ang)
    z = jnp.zeros_like(cos)
    c = jnp.concatenate([cos, cos, z, z], axis=-1)
    s1 = jnp.concatenate([-sin, z, z, z], axis=-1)
    s2 = jnp.concatenate([z, sin, z, z], axis=-1)
    return c, s1, s2


def kernel(x_prompt, x_sample, cache_ckv, cache_krope, c, c_ctx, w_ada, b_ada, g_mix, w_in, w_pool,
           pool_scale, g_q, w_uq, g_kv, w_ukv, g_sgu, w_s, b_s, w_out, g_ffn, w_ff1, w_ff2, g_final):
    dec_b = x_sample.shape[0]
    w_in_b = w_in.astype(BF16)
    w_in_p = jnp.concatenate(
        [w_in_b[:, :, :P_R + QK_ROPE], jnp.zeros((DEPTH, D, 128 - QK_ROPE), BF16), w_in_b[:, :, P_R + QK_ROPE:]],
        axis=-1)
    w_uq_h = w_uq.astype(BF16).reshape(DEPTH, Q_RANK, HEADS, QK_NOPE + QK_ROPE)
    w_uq_p = jnp.pad(w_uq_h, ((0, 0), (0, 0), (0, 0), (0, QK_PAD - QK_NOPE - QK_ROPE)))
    w_uq_p = w_uq_p.reshape(DEPTH, Q_RANK, HEADS * QK_PAD)
    w_ukv_h = w_ukv.astype(BF16).reshape(DEPTH, KV_RANK, HEADS, QK_NOPE + V_DIM)
    w_uk = w_ukv_h[..., :QK_NOPE].reshape(DEPTH, KV_RANK, HEADS * QK_NOPE)
    w_uvt = jnp.swapaxes(w_ukv_h[..., QK_NOPE:].reshape(DEPTH, KV_RANK, HEADS * V_DIM), 1, 2)
    w_pool_bd = jnp.zeros((DEPTH, POOL_W, POOL_W), BF16)
    for g in range(len(POOL_WINDOWS)):
        sl = slice(g * POOL_GD, (g + 1) * POOL_GD)
        w_pool_bd = w_pool_bd.at[:, sl, sl].set(w_pool[:, g].astype(BF16))
    weights = {
        "g_mix": g_mix.reshape(DEPTH, 1, D),
        "w_in": w_in_p,
        "g_q": g_q.reshape(DEPTH, 1, Q_RANK),
        "w_uq": w_uq_p,
        "g_kv": g_kv.reshape(DEPTH, 1, KV_RANK),
        "w_uk": w_uk,
        "w_uvt": w_uvt,
        "w_pool": w_pool_bd,
        "pool_scale": pool_scale.reshape(DEPTH, 1, POOL_W),
        "g_sgu": g_sgu.reshape(DEPTH, 1, GMLP_W),
        "w_s": w_s.reshape(DEPTH, GMLP_G * CHUNK, CHUNK).astype(BF16),
        "b_s": jnp.repeat(jnp.swapaxes(b_s, 1, 2), GMLP_GD, axis=2),
        "w_out": w_out.astype(BF16),
        "g_ffn": g_ffn.reshape(DEPTH, 1, D),
        "w_ff1": w_ff1.astype(BF16),
        "w_ff2": w_ff2.astype(BF16),
    }
    g_fin = g_final.reshape(1, D)
    rope_tabs = _rope_tables(x_sample.shape[1])

    n_rows = 16
    c_all = jnp.zeros((n_rows, D), F32).at[0].set(c_ctx).at[1:1 + dec_b].set(c)
    mod = _ada(c_all, w_ada, b_ada).reshape(DEPTH, n_rows, 1, 6 * D)

    cache_kr_pad = jnp.pad(cache_krope, ((0, 0), (0, 0), (0, 0), (0, 128 - QK_ROPE))).astype(BF16)
    cache = _cache_expand(cache_ckv, cache_kr_pad, w_uk, w_uvt)

    xp, xs = x_prompt, x_sample
    ckv_list, kr_list = [], []
    for l in range(DEPTH):
        last = l == DEPTH - 1
        q, k, vt, ypg, sckv, skr = _pre(xp, mod, (0, 0), l, weights, None, tm=256, write_state=True)
        ymla = _attn(q, k, vt, None, l, tq=256)
        xp = _post(xp, ypg, ymla, mod, (0, 0), l, weights, g_fin if last else None, tm=256)
        ckv_list.append(sckv)
        kr_list.append(skr)
        q, k, vt, ypg = _pre(xs, mod, (1, 1), l, weights, rope_tabs, tm=512, write_state=False)
        ymla = _attn(q, k, vt, cache, l, tq=512)
        xs = _post(xs, ypg, ymla, mod, (1, 1), l, weights, g_fin if last else None, tm=512)
    return xp, xs, jnp.stack(ckv_list, axis=1), jnp.stack(kr_list, axis=1)
```

```python
import functools
import math

import jax
import jax.numpy as jnp
from jax.experimental import pallas as pl
from jax.experimental.pallas import tpu as pltpu

D = 1024
DEPTH = 4
GRID_W = 64
EPS = 1e-6
POOL_W = 256
POOL_WINDOWS = (2, 4, 8, 16)
POOL_GD = 64
HALO = 8
QK_NOPE = 128
QK_ROPE = 64
V_DIM = 128
V_AUG = V_DIM + 16
HEADS = 4
Q_RANK = 384
KV_RANK = 256
ROPE_THETA = 10000.0
CHUNK = 128
GMLP_W = 256
GMLP_G = 4
GMLP_GD = 64
D_FF = 4096
FF_CHUNK = 1024
P_Q = 256
P_KV = P_Q + Q_RANK
P_R = P_KV + KV_RANK
P_G = P_R + 128
P_END = P_G + 2 * GMLP_W
QK_PAD = 256
Q_UNIT = 256
LOOKAHEAD = 2
SM_SCALE = math.log2(math.e) / math.sqrt(QK_NOPE + QK_ROPE)
NT_DIMS = (((1,), (1,)), ((), ()))

F32 = jnp.float32
BF16 = jnp.bfloat16
MIB = 1024 * 1024


def _params(vmem_mib):
    return pltpu.CompilerParams(
        dimension_semantics=("arbitrary", "arbitrary"),
        vmem_limit_bytes=vmem_mib * MIB,
    )


def _rms(x, g):
    y = x * jax.lax.rsqrt(jnp.mean(x * x, axis=-1, keepdims=True) + EPS)
    return y * g


def _ada_rms(x, g, scale, shift):
    y = x * jax.lax.rsqrt(jnp.mean(x * x, axis=-1, keepdims=True) + EPS)
    return y * (g * (1.0 + scale)) + shift


def _dot(a, b):
    return jnp.dot(a, b, preferred_element_type=F32)


def _dot_nt(a, b):
    return jax.lax.dot_general(a, b, NT_DIMS, preferred_element_type=F32)


def _layer_spec(shape, layer):
    return pl.BlockSpec((None,) + shape, lambda bi, i: (layer,) + (0,) * len(shape))


def _ada_kernel(c_ref, w_ref, b_ref, o_ref):
    c = c_ref[...]
    s = c / (1.0 + jnp.exp(-c))
    o_ref[0] = _dot(s.astype(BF16), w_ref[0].astype(BF16)) + b_ref[0]


def _ada(c_all, w_ada, b_ada):
    rows = c_all.shape[0]
    tn = 1536
    return pl.pallas_call(
        _ada_kernel,
        grid=(DEPTH, 6 * D // tn),
        in_specs=[
            pl.BlockSpec((rows, D), lambda l, j: (0, 0)),
            pl.BlockSpec((1, D, tn), lambda l, j: (l, 0, j)),
            pl.BlockSpec((1, 1, tn), lambda l, j: (l, 0, j)),
        ],
        out_specs=pl.BlockSpec((1, rows, tn), lambda l, j: (l, 0, j)),
        out_shape=jax.ShapeDtypeStruct((DEPTH, rows, 6 * D), F32),
        compiler_params=_params(32),
        name="ada",
    )(c_all, w_ada, b_ada.reshape(DEPTH, 1, 6 * D))


def _cache_kernel(ckv_ref, kr_ref, wk_ref, wvt_ref, k_ref, vt_ref):
    ckv = ckv_ref[0, 0].astype(BF16)
    kn = _dot(ckv, wk_ref[...])
    vt = _dot_nt(wvt_ref[...], ckv)
    kr = kr_ref[0, 0]
    for h in range(HEADS):
        k_ref[0, 0, h] = jnp.concatenate([kn[:, h * QK_NOPE:(h + 1) * QK_NOPE].astype(BF16), kr], axis=1)
        vt_ref[0, 0, h, 0:V_DIM] = vt[h * V_DIM:(h + 1) * V_DIM].astype(BF16)
        vt_ref[0, 0, h, V_DIM:V_AUG] = jnp.ones((V_AUG - V_DIM, vt.shape[1]), BF16)


def _cache_expand(cache_ckv, cache_kr_pad, w_uk, w_uvt):
    b, depth, p, _ = cache_ckv.shape
    return pl.pallas_call(
        _cache_kernel,
        grid=(b, depth),
        in_specs=[
            pl.BlockSpec((1, 1, p, KV_RANK), lambda i, l: (i, l, 0, 0)),
            pl.BlockSpec((1, 1, p, 128), lambda i, l: (i, l, 0, 0)),
            pl.BlockSpec((None, KV_RANK, HEADS * QK_NOPE), lambda i, l: (l, 0, 0)),
            pl.BlockSpec((None, HEADS * V_DIM, KV_RANK), lambda i, l: (l, 0, 0)),
        ],
        out_specs=[
            pl.BlockSpec((1, 1, HEADS, p, QK_PAD), lambda i, l: (i, l, 0, 0, 0)),
            pl.BlockSpec((1, 1, HEADS, V_AUG, p), lambda i, l: (i, l, 0, 0, 0)),
        ],
        out_shape=[
            jax.ShapeDtypeStruct((b, depth, HEADS, p, QK_PAD), BF16),
            jax.ShapeDtypeStruct((b, depth, HEADS, V_AUG, p), BF16),
        ],
        compiler_params=_params(32),
        name="cache_expand",
    )(cache_ckv, cache_kr_pad, w_uk, w_uvt)


def _shift_up(a, k):
    return pltpu.roll(a, a.shape[0] - k, axis=0)


def _rope(x, c, s1, s2):
    return x * c + pltpu.roll(x, 96, axis=1) * s1 + pltpu.roll(x, 32, axis=1) * s2


def _pre_kernel(*refs, tm, seq, use_rope, write_state):
    it = iter(refs)
    x_ref, xp_ref, xn_ref, mod_ref = next(it), next(it), next(it), next(it)
    gmix_ref, win_ref, wing_ref = next(it), next(it), next(it)
    gq_ref, wuq_ref, gkv_ref, wuk_ref, wuvt_ref = next(it), next(it), next(it), next(it), next(it)
    wpool_ref, pscale_ref = next(it), next(it)
    gsgu_ref, ws_ref, bs_ref = next(it), next(it), next(it)
    if use_rope:
        rc_ref, rs1_ref, rs2_ref = next(it), next(it), next(it)
    q_ref, k_ref, vt_ref, ypg_ref = next(it), next(it), next(it), next(it)
    if write_state:
        sckv_ref, skr_ref = next(it), next(it)

    i = pl.program_id(1)
    rows = tm + 2 * HALO
    xe = jnp.concatenate([xp_ref[0], x_ref[0], xn_ref[0]], axis=0)
    sh1 = mod_ref[:, 0:D]
    sc1 = mod_ref[:, D:2 * D]
    h = _ada_rms(xe, gmix_ref[...], sc1, sh1).astype(BF16)
    proj_e = _dot(h, win_ref[...])
    proj = proj_e[HALO:HALO + tm]
    proj_g = _dot(h, wing_ref[...])[HALO:HALO + tm]

    pos = i * tm - HALO + jax.lax.broadcasted_iota(jnp.int32, (rows, 1), 0)
    hp_e = jnp.where((pos >= 0) & (pos < seq), proj_e[:, 0:POOL_W], 0.0)
    a2 = hp_e + _shift_up(hp_e, 1)
    a4 = a2 + _shift_up(a2, 2)
    a8 = a4 + _shift_up(a4, 4)
    a16 = a8 + _shift_up(a8, 8)
    win2 = _shift_up(a2, HALO - 1)[:tm]
    win4 = _shift_up(a4, HALO - 2)[:tm]
    win8 = _shift_up(a8, HALO - 4)[:tm]
    win16 = a16[:tm]
    grp = jax.lax.broadcasted_iota(jnp.int32, (1, POOL_W), 1) // POOL_GD
    win = jnp.where(grp == 0, win2, jnp.where(grp == 1, win4, jnp.where(grp == 2, win8, win16)))
    half = jnp.where(grp == 0, 1, jnp.where(grp == 1, 2, jnp.where(grp == 2, 4, 8)))
    t = i * tm + jax.lax.broadcasted_iota(jnp.int32, (tm, 1), 0)
    cnt = jnp.clip(t + half, 0, seq) - jnp.clip(t - half, 0, seq)
    pooled = win / cnt.astype(F32) - proj[:, 0:POOL_W]
    y_pool = _dot(pooled.astype(BF16), wpool_ref[...]) * pscale_ref[...]

    cq = _rms(proj[:, P_Q:P_KV], gq_ref[...])
    q = _dot(cq.astype(BF16), wuq_ref[...])
    ckv = _rms(proj[:, P_KV:P_R], gkv_ref[...])
    ckv_b = ckv.astype(BF16)
    kn = _dot(ckv_b, wuk_ref[...])
    vt = _dot_nt(wuvt_ref[...], ckv_b)
    kr = proj[:, P_R:P_G]
    if write_state:
        sckv_ref[0] = ckv
        skr_ref[0] = kr[:, 0:QK_ROPE]
    if use_rope:
        rc, rs1, rs2 = rc_ref[...], rs1_ref[...], rs2_ref[...]
        kr = _rope(kr, rc, rs1, rs2)
    kr_b = kr.astype(BF16)
    for hd in range(HEADS):
        qn = q[:, hd * QK_PAD:hd * QK_PAD + QK_NOPE]
        qr = q[:, hd * QK_PAD + QK_NOPE:(hd + 1) * QK_PAD]
        if use_rope:
            qr = _rope(qr, rc, rs1, rs2)
        q_ref[0, hd] = jnp.concatenate([qn, qr], axis=1).astype(BF16)
        k_ref[0, hd] = jnp.concatenate([kn[:, hd * QK_NOPE:(hd + 1) * QK_NOPE].astype(BF16), kr_b], axis=1)
        vt_ref[0, hd, 0:V_DIM] = vt[hd * V_DIM:(hd + 1) * V_DIM].astype(BF16)
        vt_ref[0, hd, V_DIM:V_AUG] = jnp.ones((V_AUG - V_DIM, tm), BF16)

    uv = jax.nn.gelu(proj_g)
    u = uv[:, 0:GMLP_W]
    vg = _rms(uv[:, GMLP_W:], gsgu_ref[...]).astype(BF16)
    ggrp = jax.lax.broadcasted_iota(jnp.int32, (1, GMLP_W), 1) // GMLP_GD
    ys = []
    for cix in range(tm // CHUNK):
        r = _dot(ws_ref[...], vg[cix * CHUNK:(cix + 1) * CHUNK])
        mixed = jnp.where(
            ggrp == 0, r[0:CHUNK],
            jnp.where(ggrp == 1, r[CHUNK:2 * CHUNK],
                      jnp.where(ggrp == 2, r[2 * CHUNK:3 * CHUNK], r[3 * CHUNK:4 * CHUNK])))
        ys.append(u[cix * CHUNK:(cix + 1) * CHUNK] * (mixed + bs_ref[...]))
    y_g = jnp.concatenate(ys, axis=0)
    ypg_ref[0] = jnp.concatenate([y_pool, y_g], axis=1).astype(BF16)


def _pre(x, mod, mod_off, layer, w, rope_tabs, *, tm, write_state):
    b, seq, _ = x.shape
    nt = seq // tm
    use_rope = rope_tabs is not None
    hb = tm // HALO
    nhb = seq // HALO
    ls = functools.partial(_layer_spec, layer=layer)

    in_specs = [
        pl.BlockSpec((1, tm, D), lambda bi, i: (bi, i, 0)),
        pl.BlockSpec((1, HALO, D), lambda bi, i: (bi, jnp.maximum(i * hb - 1, 0), 0)),
        pl.BlockSpec((1, HALO, D), lambda bi, i: (bi, jnp.minimum((i + 1) * hb, nhb - 1), 0)),
        pl.BlockSpec((None, None, 1, 6 * D), lambda bi, i: (layer, bi * mod_off[1] + mod_off[0], 0, 0)),
        ls((1, D)), ls((D, P_G)), ls((D, 2 * GMLP_W)),
        ls((1, Q_RANK)), ls((Q_RANK, HEADS * QK_PAD)),
        ls((1, KV_RANK)), ls((KV_RANK, HEADS * QK_NOPE)), ls((HEADS * V_DIM, KV_RANK)),
        ls((POOL_W, POOL_W)), ls((1, POOL_W)),
        ls((1, GMLP_W)), ls((GMLP_G * CHUNK, CHUNK)), ls((CHUNK, GMLP_W)),
    ]
    args = [x, x, x, mod, w["g_mix"], w["w_in"], w["w_in_g"], w["g_q"], w["w_uq"], w["g_kv"], w["w_uk"], w["w_uvt"],
            w["w_pool"], w["pool_scale"], w["g_sgu"], w["w_s"], w["b_s"]]
    if use_rope:
        in_specs += [pl.BlockSpec((tm, 128), lambda bi, i: (i, 0))] * 3
        args += list(rope_tabs)
    out_specs = [
        pl.BlockSpec((1, HEADS, tm, QK_PAD), lambda bi, i: (bi, 0, i, 0)),
        pl.BlockSpec((1, HEADS, tm, QK_PAD), lambda bi, i: (bi, 0, i, 0)),
        pl.BlockSpec((1, HEADS, V_AUG, tm), lambda bi, i: (bi, 0, 0, i)),
        pl.BlockSpec((1, tm, 2 * GMLP_W), lambda bi, i: (bi, i, 0)),
    ]
    out_shape = [
        jax.ShapeDtypeStruct((b, HEADS, seq, QK_PAD), BF16),
        jax.ShapeDtypeStruct((b, HEADS, seq, QK_PAD), BF16),
        jax.ShapeDtypeStruct((b, HEADS, V_AUG, seq), BF16),
        jax.ShapeDtypeStruct((b, seq, 2 * GMLP_W), BF16),
    ]
    if write_state:
        out_specs += [
            pl.BlockSpec((1, tm, KV_RANK), lambda bi, i: (bi, i, 0)),
            pl.BlockSpec((1, tm, QK_ROPE), lambda bi, i: (bi, i, 0)),
        ]
        out_shape += [
            jax.ShapeDtypeStruct((b, seq, KV_RANK), F32),
            jax.ShapeDtypeStruct((b, seq, QK_ROPE), F32),
        ]
    return pl.pallas_call(
        functools.partial(_pre_kernel, tm=tm, seq=seq, use_rope=use_rope, write_state=write_state),
        grid=(b, nt),
        in_specs=in_specs,
        out_specs=out_specs,
        out_shape=out_shape,
        compiler_params=_params(48),
        name="pre_lat" if use_rope else "pre_ctx",
    )(*args)


def _attn_kernel(*refs, tq, with_cache):
    if with_cache:
        q_ref, k_ref, vt_ref, kc_ref, vct_ref, o_ref = refs
    else:
        q_ref, k_ref, vt_ref, o_ref = refs
    qu = min(Q_UNIT, tq)
    units = [(h, slice(u * qu, (u + 1) * qu)) for h in range(HEADS) for u in range(tq // qu)]

    def scores(unit):
        h, rows = unit
        q = q_ref[0, h, rows, :]
        st = _dot_nt(k_ref[0, h], q)
        sct = _dot_nt(kc_ref[0, 0, h], q) if with_cache else None
        return st, sct

    ahead = [scores(u) for u in units[:LOOKAHEAD]]
    for idx, (h, rows) in enumerate(units):
        st, sct = ahead.pop(0)
        if idx + LOOKAHEAD < len(units):
            ahead.append(scores(units[idx + LOOKAHEAD]))
        m = jnp.max(st, axis=0, keepdims=True)
        if with_cache:
            m = jnp.maximum(m, jnp.max(sct, axis=0, keepdims=True))
        ot = _dot(vt_ref[0, h], jnp.exp2(st - m).astype(BF16))
        if with_cache:
            ot = ot + _dot(vct_ref[0, 0, h], jnp.exp2(sct - m).astype(BF16))
        o = ot[0:V_DIM] / ot[V_DIM:V_DIM + 1]
        o_ref[0, rows, h * V_DIM:(h + 1) * V_DIM] = o.T.astype(BF16)


def _attn(q, k, vt, cache, layer, *, tq):
    b, _, seq, _ = q.shape
    with_cache = cache is not None
    in_specs = [
        pl.BlockSpec((1, HEADS, tq, QK_PAD), lambda bi, i: (bi, 0, i, 0)),
        pl.BlockSpec((1, HEADS, seq, QK_PAD), lambda bi, i: (bi, 0, 0, 0)),
        pl.BlockSpec((1, HEADS, V_AUG, seq), lambda bi, i: (bi, 0, 0, 0)),
    ]
    args = [q, k, vt]
    if with_cache:
        kc, vct = cache
        p = kc.shape[3]
        in_specs += [
            pl.BlockSpec((1, 1, HEADS, p, QK_PAD), lambda bi, i: (bi, layer, 0, 0, 0)),
            pl.BlockSpec((1, 1, HEADS, V_AUG, p), lambda bi, i: (bi, layer, 0, 0, 0)),
        ]
        args += [kc, vct]
    return pl.pallas_call(
        functools.partial(_attn_kernel, tq=tq, with_cache=with_cache),
        grid=(b, seq // tq),
        in_specs=in_specs,
        out_specs=pl.BlockSpec((1, tq, HEADS * V_DIM), lambda bi, i: (bi, i, 0)),
        out_shape=jax.ShapeDtypeStruct((b, seq, HEADS * V_DIM), BF16),
        compiler_params=_params(48),
        name="attn_lat" if with_cache else "attn_ctx",
    )(*args)


def _post_kernel(*refs, final):
    if final:
        x_ref, ypg_ref, ymla_ref, mod_ref, wout_ref, gffn_ref, w1_ref, w2_ref, gfin_ref, o_ref = refs
    else:
        x_ref, ypg_ref, ymla_ref, mod_ref, wout_ref, gffn_ref, w1_ref, w2_ref, o_ref = refs
    g1 = mod_ref[:, 2 * D:3 * D]
    sh2 = mod_ref[:, 3 * D:4 * D]
    sc2 = mod_ref[:, 4 * D:5 * D]
    g2 = mod_ref[:, 5 * D:6 * D]
    ypg = ypg_ref[0]
    ycat = jnp.concatenate([ypg[:, 0:POOL_W], ymla_ref[0], ypg[:, POOL_W:]], axis=1)
    x1 = x_ref[0] + g1 * _dot(ycat, wout_ref[...])
    h = _ada_rms(x1, gffn_ref[...], sc2, sh2).astype(BF16)
    acc = None
    for j in range(D_FF // FF_CHUNK):
        a = _dot(h, w1_ref[:, j * FF_CHUNK:(j + 1) * FF_CHUNK])
        a = jnp.square(jnp.maximum(a, 0.0)).astype(BF16)
        part = _dot(a, w2_ref[j * FF_CHUNK:(j + 1) * FF_CHUNK, :])
        acc = part if acc is None else acc + part
    out = x1 + g2 * acc
    if final:
        out = _rms(out, gfin_ref[...])
    o_ref[0] = out


def _post(x, ypg, ymla, mod, mod_off, layer, w, g_final, *, tm):
    b, seq, _ = x.shape
    final = g_final is not None

    def big(shape):
        return pl.BlockSpec((None,) + shape, lambda bi, i: (layer,) + (0,) * len(shape),
                            pipeline_mode=pl.Buffered(1))

    in_specs = [
        pl.BlockSpec((1, tm, D), lambda bi, i: (bi, i, 0)),
        pl.BlockSpec((1, tm, 2 * GMLP_W), lambda bi, i: (bi, i, 0)),
        pl.BlockSpec((1, tm, HEADS * V_DIM), lambda bi, i: (bi, i, 0)),
        pl.BlockSpec((None, None, 1, 6 * D), lambda bi, i: (layer, bi * mod_off[1] + mod_off[0], 0, 0)),
        big((D, D)), _layer_spec((1, D), layer), big((D, D_FF)), big((D_FF, D)),
    ]
    args = [x, ypg, ymla, mod, w["w_out"], w["g_ffn"], w["w_ff1"], w["w_ff2"]]
    if final:
        in_specs.append(pl.BlockSpec((1, D), lambda bi, i: (0, 0)))
        args.append(g_final)
    return pl.pallas_call(
        functools.partial(_post_kernel, final=final),
        grid=(b, seq // tm),
        in_specs=in_specs,
        out_specs=pl.BlockSpec((1, tm, D), lambda bi, i: (bi, i, 0)),
        out_shape=jax.ShapeDtypeStruct((b, seq, D), F32),
        compiler_params=_params(56),
        name="post",
    )(*args)


def _rope_tables(seq):
    rows = seq // GRID_W
    row = jnp.repeat(jnp.arange(rows, dtype=F32), GRID_W)
    col = jnp.tile(jnp.arange(GRID_W, dtype=F32), rows)
    n_freq = QK_ROPE // 4
    inv = 1.0 / (ROPE_THETA ** (jnp.arange(n_freq, dtype=F32) / n_freq))
    ang = jnp.concatenate([row[:, None] * inv, col[:, None] * inv], axis=-1)
    cos, sin = jnp.cos(ang), jnp.sin(ang)
    z = jnp.zeros_like(cos)
    c = jnp.concatenate([cos, cos, z, z], axis=-1)
    s1 = jnp.concatenate([-sin, z, z, z], axis=-1)
    s2 = jnp.concatenate([z, sin, z, z], axis=-1)
    return c, s1, s2


def _pair_up(a):
    return a.reshape(a.shape[0] // 2, 2 * a.shape[1], a.shape[2])


def kernel(x_prompt, x_sample, cache_ckv, cache_krope, c, c_ctx, w_ada, b_ada, g_mix, w_in, w_pool,
           pool_scale, g_q, w_uq, g_kv, w_ukv, g_sgu, w_s, b_s, w_out, g_ffn, w_ff1, w_ff2, g_final):
    dec_b = x_sample.shape[0]
    w_in_p = jnp.pad(w_in[:, :, :P_R + QK_ROPE].astype(BF16), ((0, 0), (0, 0), (0, 128 - QK_ROPE)))
    w_in_g = w_in[:, :, P_R + QK_ROPE:].astype(BF16)
    w_uq_h = w_uq.astype(BF16).reshape(DEPTH, Q_RANK, HEADS, QK_NOPE + QK_ROPE)
    w_uq_p = jnp.pad(w_uq_h, ((0, 0), (0, 0), (0, 0), (0, QK_PAD - QK_NOPE - QK_ROPE)))
    w_uq_p = w_uq_p.reshape(DEPTH, Q_RANK, HEADS * QK_PAD)
    w_ukv_h = w_ukv.astype(BF16).reshape(DEPTH, KV_RANK, HEADS, QK_NOPE + V_DIM)
    w_uk = w_ukv_h[..., :QK_NOPE].reshape(DEPTH, KV_RANK, HEADS * QK_NOPE)
    w_uvt = jnp.swapaxes(w_ukv_h[..., QK_NOPE:].reshape(DEPTH, KV_RANK, HEADS * V_DIM), 1, 2)
    w_pool_bd = jnp.zeros((DEPTH, POOL_W, POOL_W), BF16)
    for g in range(len(POOL_WINDOWS)):
        sl = slice(g * POOL_GD, (g + 1) * POOL_GD)
        w_pool_bd = w_pool_bd.at[:, sl, sl].set(w_pool[:, g].astype(BF16))
    weights = {
        "g_mix": g_mix.reshape(DEPTH, 1, D),
        "w_in": w_in_p,
        "w_in_g": w_in_g,
        "g_q": (g_q * SM_SCALE).reshape(DEPTH, 1, Q_RANK),
        "w_uq": w_uq_p,
        "g_kv": g_kv.reshape(DEPTH, 1, KV_RANK),
        "w_uk": w_uk,
        "w_uvt": w_uvt,
        "w_pool": w_pool_bd,
        "pool_scale": pool_scale.reshape(DEPTH, 1, POOL_W),
        "g_sgu": g_sgu.reshape(DEPTH, 1, GMLP_W),
        "w_s": w_s.reshape(DEPTH, GMLP_G * CHUNK, CHUNK).astype(BF16),
        "b_s": jnp.repeat(jnp.swapaxes(b_s, 1, 2), GMLP_GD, axis=2),
        "w_out": w_out.astype(BF16),
        "g_ffn": g_ffn.reshape(DEPTH, 1, D),
        "w_ff1": w_ff1.astype(BF16),
        "w_ff2": w_ff2.astype(BF16),
    }
    g_fin = g_final.reshape(1, D)
    rope_tabs = _rope_tables(x_sample.shape[1])

    n_rows = 16
    c_all = jnp.zeros((n_rows, D), F32).at[0].set(c_ctx).at[1:1 + dec_b].set(c)
    mod = _ada(c_all, w_ada, b_ada).reshape(DEPTH, n_rows, 1, 6 * D)

    cache_kr_pad = jnp.pad(cache_krope, ((0, 0), (0, 0), (0, 0), (0, 128 - QK_ROPE))).astype(BF16)
    cache = _cache_expand(cache_ckv, cache_kr_pad, w_uk, w_uvt)

    xp, xs = x_prompt, x_sample
    ckv_list, kr_list = [], []
    for l in range(DEPTH):
        last = l == DEPTH - 1
        q, k, vt, ypg, sckv, skr = _pre(xp, mod, (0, 0), l, weights, None, tm=256, write_state=True)
        ymla = _attn(q, k, vt, None, l, tq=256)
        xp = _post(_pair_up(xp), _pair_up(ypg), _pair_up(ymla), mod, (0, 0), l, weights,
                   g_fin if last else None, tm=512).reshape(xp.shape)
        ckv_list.append(sckv)
        kr_list.append(skr)
        q, k, vt, ypg = _pre(xs, mod, (1, 1), l, weights, rope_tabs, tm=512, write_state=False)
        ymla = _attn(q, k, vt, cache, l, tq=1024)
        xs = _post(xs, ypg, ymla, mod, (1, 1), l, weights, g_fin if last else None, tm=512)
    return xp, xs, jnp.stack(ckv_list, axis=1), jnp.stack(kr_list, axis=1)
```

```python
import functools
import math

import jax
import jax.numpy as jnp
from jax.experimental import pallas as pl
from jax.experimental.pallas import tpu as pltpu

D = 1024
DEPTH = 4
GRID_W = 64
EPS = 1e-6
POOL_W = 256
POOL_WINDOWS = (2, 4, 8, 16)
POOL_GD = 64
HALO = 8
QK_NOPE = 128
QK_ROPE = 64
V_DIM = 128
V_AUG = V_DIM + 16
HEADS = 4
Q_RANK = 384
KV_RANK = 256
ROPE_THETA = 10000.0
CHUNK = 128
GMLP_W = 256
GMLP_G = 4
GMLP_GD = 64
D_FF = 4096
FF_CHUNK = 1024
P_Q = 256
P_KV = P_Q + Q_RANK
P_R = P_KV + KV_RANK
P_G = P_R + 128
QK_PAD = 256
Q_UNIT = 256
CTX_GROUP = 2
LOOKAHEAD = 2
SM_SCALE = math.log2(math.e) / math.sqrt(QK_NOPE + QK_ROPE)
NT_DIMS = (((1,), (1,)), ((), ()))

F32 = jnp.float32
BF16 = jnp.bfloat16
MIB = 1024 * 1024

MIXER_PARAMS = ("g_mix", "w_in", "w_in_g", "g_q", "w_uq", "g_kv", "w_uk", "w_uvt",
                "w_pool", "pool_scale", "g_sgu", "w_s", "b_s")
MIXER_SHAPES = {
    "g_mix": (1, D), "w_in": (D, P_G), "w_in_g": (D, 2 * GMLP_W), "g_q": (1, Q_RANK),
    "w_uq": (Q_RANK, HEADS * QK_PAD), "g_kv": (1, KV_RANK), "w_uk": (KV_RANK, HEADS * QK_NOPE),
    "w_uvt": (HEADS * V_DIM, KV_RANK), "w_pool": (POOL_W, POOL_W), "pool_scale": (1, POOL_W),
    "g_sgu": (1, GMLP_W), "w_s": (GMLP_G * CHUNK, CHUNK), "b_s": (CHUNK, GMLP_W),
}
FFN_PARAMS = ("w_out", "g_ffn", "w_ff1", "w_ff2")
FFN_SHAPES = {"w_out": (D, D), "g_ffn": (1, D), "w_ff1": (D, D_FF), "w_ff2": (D_FF, D)}
SINGLE_BUFFERED = ("w_in", "w_out", "w_ff1", "w_ff2")


def _params(vmem_mib):
    return pltpu.CompilerParams(
        dimension_semantics=("arbitrary", "arbitrary"),
        vmem_limit_bytes=vmem_mib * MIB,
    )


def _rms(x, g):
    y = x * jax.lax.rsqrt(jnp.mean(x * x, axis=-1, keepdims=True) + EPS)
    return y * g


def _ada_rms(x, g, scale, shift):
    y = x * jax.lax.rsqrt(jnp.mean(x * x, axis=-1, keepdims=True) + EPS)
    return y * (g * (1.0 + scale)) + shift


def _dot(a, b):
    return jnp.dot(a, b, preferred_element_type=F32)


def _dot_nt(a, b):
    return jax.lax.dot_general(a, b, NT_DIMS, preferred_element_type=F32)


def _layer_spec(shape, layer, single=False):
    index_map = lambda bi, i: (layer,) + (0,) * len(shape)
    if single:
        return pl.BlockSpec((None,) + shape, index_map, pipeline_mode=pl.Buffered(1))
    return pl.BlockSpec((None,) + shape, index_map)


def _param_specs(names, shapes, layer):
    return [_layer_spec(shapes[n], layer, n in SINGLE_BUFFERED) for n in names]


def _mod_spec(layer, row0, per_batch):
    return pl.BlockSpec((None, None, 1, 6 * D), lambda bi, i: (layer, bi * per_batch + row0, 0, 0))


def _ada_kernel(c_ref, w_ref, b_ref, o_ref):
    c = c_ref[...]
    s = c / (1.0 + jnp.exp(-c))
    o_ref[0] = _dot(s.astype(BF16), w_ref[0].astype(BF16)) + b_ref[0]


def _ada(c_all, w_ada, b_ada):
    rows = c_all.shape[0]
    tn = 1536
    return pl.pallas_call(
        _ada_kernel,
        grid=(DEPTH, 6 * D // tn),
        in_specs=[
            pl.BlockSpec((rows, D), lambda l, j: (0, 0)),
            pl.BlockSpec((1, D, tn), lambda l, j: (l, 0, j)),
            pl.BlockSpec((1, 1, tn), lambda l, j: (l, 0, j)),
        ],
        out_specs=pl.BlockSpec((1, rows, tn), lambda l, j: (l, 0, j)),
        out_shape=jax.ShapeDtypeStruct((DEPTH, rows, 6 * D), F32),
        compiler_params=_params(32),
        name="ada",
    )(c_all, w_ada, b_ada.reshape(DEPTH, 1, 6 * D))


def _cache_kernel(ckv_ref, kr_ref, wk_ref, wvt_ref, k_ref, vt_ref):
    ckv = ckv_ref[0, 0].astype(BF16)
    kn = _dot(ckv, wk_ref[...])
    vt = _dot_nt(wvt_ref[...], ckv)
    kr = kr_ref[0, 0]
    for h in range(HEADS):
        k_ref[0, 0, h] = jnp.concatenate([kn[:, h * QK_NOPE:(h + 1) * QK_NOPE].astype(BF16), kr], axis=1)
        vt_ref[0, 0, h, 0:V_DIM] = vt[h * V_DIM:(h + 1) * V_DIM].astype(BF16)
        vt_ref[0, 0, h, V_DIM:V_AUG] = jnp.ones((V_AUG - V_DIM, vt.shape[1]), BF16)


def _cache_expand(cache_ckv, cache_kr_pad, w_uk, w_uvt):
    b, depth, p, _ = cache_ckv.shape
    return pl.pallas_call(
        _cache_kernel,
        grid=(b, depth),
        in_specs=[
            pl.BlockSpec((1, 1, p, KV_RANK), lambda i, l: (i, l, 0, 0)),
            pl.BlockSpec((1, 1, p, 128), lambda i, l: (i, l, 0, 0)),
            pl.BlockSpec((None, KV_RANK, HEADS * QK_NOPE), lambda i, l: (l, 0, 0)),
            pl.BlockSpec((None, HEADS * V_DIM, KV_RANK), lambda i, l: (l, 0, 0)),
        ],
        out_specs=[
            pl.BlockSpec((1, 1, HEADS, p, QK_PAD), lambda i, l: (i, l, 0, 0, 0)),
            pl.BlockSpec((1, 1, HEADS, V_AUG, p), lambda i, l: (i, l, 0, 0, 0)),
        ],
        out_shape=[
            jax.ShapeDtypeStruct((b, depth, HEADS, p, QK_PAD), BF16),
            jax.ShapeDtypeStruct((b, depth, HEADS, V_AUG, p), BF16),
        ],
        compiler_params=_params(32),
        name="cache_expand",
    )(cache_ckv, cache_kr_pad, w_uk, w_uvt)


def _shift_up(a, k):
    return pltpu.roll(a, a.shape[0] - k, axis=0)


def _rope(x, c, s1, s2):
    return x * c + pltpu.roll(x, 96, axis=1) * s1 + pltpu.roll(x, 32, axis=1) * s2


def _pool_mix(hp_e, hp, t0, seq, r):
    tm = hp.shape[0]
    a2 = hp_e + _shift_up(hp_e, 1)
    a4 = a2 + _shift_up(a2, 2)
    a8 = a4 + _shift_up(a4, 4)
    a16 = a8 + _shift_up(a8, 8)
    win2 = _shift_up(a2, HALO - 1)[:tm]
    win4 = _shift_up(a4, HALO - 2)[:tm]
    win8 = _shift_up(a8, HALO - 4)[:tm]
    win16 = a16[:tm]
    grp = jax.lax.broadcasted_iota(jnp.int32, (1, POOL_W), 1) // POOL_GD
    win = jnp.where(grp == 0, win2, jnp.where(grp == 1, win4, jnp.where(grp == 2, win8, win16)))
    half = jnp.where(grp == 0, 1, jnp.where(grp == 1, 2, jnp.where(grp == 2, 4, 8)))
    t = t0 + jax.lax.broadcasted_iota(jnp.int32, (tm, 1), 0)
    cnt = jnp.clip(t + half, 0, seq) - jnp.clip(t - half, 0, seq)
    pooled = win / cnt.astype(F32) - hp
    return _dot(pooled.astype(BF16), r["w_pool"][...]) * r["pool_scale"][...]


def _gmlp(proj_g, r):
    tm = proj_g.shape[0]
    uv = jax.nn.gelu(proj_g)
    u = uv[:, 0:GMLP_W]
    vg = _rms(uv[:, GMLP_W:], r["g_sgu"][...]).astype(BF16)
    ggrp = jax.lax.broadcasted_iota(jnp.int32, (1, GMLP_W), 1) // GMLP_GD
    ys = []
    for cix in range(tm // CHUNK):
        m = _dot(r["w_s"][...], vg[cix * CHUNK:(cix + 1) * CHUNK])
        mixed = jnp.where(
            ggrp == 0, m[0:CHUNK],
            jnp.where(ggrp == 1, m[CHUNK:2 * CHUNK],
                      jnp.where(ggrp == 2, m[2 * CHUNK:3 * CHUNK], m[3 * CHUNK:4 * CHUNK])))
        ys.append(u[cix * CHUNK:(cix + 1) * CHUNK] * (mixed + r["b_s"][...]))
    return jnp.concatenate(ys, axis=0)


def _mla_operands(proj, r, rope):
    cq = _rms(proj[:, P_Q:P_KV], r["g_q"][...])
    q = _dot(cq.astype(BF16), r["w_uq"][...])
    ckv = _rms(proj[:, P_KV:P_R], r["g_kv"][...])
    ckv_b = ckv.astype(BF16)
    kn = _dot(ckv_b, r["w_uk"][...])
    vt = _dot_nt(r["w_uvt"][...], ckv_b)
    kr_raw = proj[:, P_R:P_G]
    kr = kr_raw if rope is None else _rope(kr_raw, *rope)
    kr_b = kr.astype(BF16)
    qs, ks, vts = [], [], []
    for hd in range(HEADS):
        qn = q[:, hd * QK_PAD:hd * QK_PAD + QK_NOPE]
        qr = q[:, hd * QK_PAD + QK_NOPE:(hd + 1) * QK_PAD]
        if rope is not None:
            qr = _rope(qr, *rope)
        qs.append(jnp.concatenate([qn, qr], axis=1).astype(BF16))
        ks.append(jnp.concatenate([kn[:, hd * QK_NOPE:(hd + 1) * QK_NOPE].astype(BF16), kr_b], axis=1))
        vts.append(vt[hd * V_DIM:(hd + 1) * V_DIM].astype(BF16))
    return qs, ks, vts, ckv, kr_raw


def _softmax_pv(st, sct, vt, vct):
    m = jnp.max(st, axis=0, keepdims=True)
    if sct is not None:
        m = jnp.maximum(m, jnp.max(sct, axis=0, keepdims=True))
    ot = _dot(vt, jnp.exp2(st - m).astype(BF16))
    if sct is not None:
        ot = ot + _dot(vct, jnp.exp2(sct - m).astype(BF16))
    return (ot[0:V_DIM] / ot[V_DIM:V_DIM + 1]).T


def _ffn_block(x, ycat, mod_ref, r, g_final):
    g1 = mod_ref[:, 2 * D:3 * D]
    sh2 = mod_ref[:, 3 * D:4 * D]
    sc2 = mod_ref[:, 4 * D:5 * D]
    g2 = mod_ref[:, 5 * D:6 * D]
    x1 = x + g1 * _dot(ycat, r["w_out"][...])
    h = _ada_rms(x1, r["g_ffn"][...], sc2, sh2).astype(BF16)
    acc = None
    for j in range(D_FF // FF_CHUNK):
        a = _dot(h, r["w_ff1"][:, j * FF_CHUNK:(j + 1) * FF_CHUNK])
        a = jnp.square(jnp.maximum(a, 0.0)).astype(BF16)
        part = _dot(a, r["w_ff2"][j * FF_CHUNK:(j + 1) * FF_CHUNK, :])
        acc = part if acc is None else acc + part
    out = x1 + g2 * acc
    if g_final is not None:
        out = _rms(out, g_final)
    return out


def _ctx_kernel(*refs, names, final):
    r = dict(zip(names, refs))
    mod_ref = r["mod"]
    group, seq, _ = r["x"].shape
    edge = jnp.zeros((HALO, POOL_W), F32)
    ones = jnp.ones((V_AUG - V_DIM, seq), BF16)
    xs, ycats = [], []
    for s in range(group):
        x = r["x"][s]
        h = _ada_rms(x, r["g_mix"][...], mod_ref[:, D:2 * D], mod_ref[:, 0:D]).astype(BF16)
        proj = _dot(h, r["w_in"][...])
        proj_g = _dot(h, r["w_in_g"][...])
        hp = proj[:, 0:POOL_W]
        y_pool = _pool_mix(jnp.concatenate([edge, hp, edge], axis=0), hp, 0, seq, r)
        y_g = _gmlp(proj_g, r)
        qs, ks, vts, ckv, kr_raw = _mla_operands(proj, r, None)
        r["sckv"][s] = ckv
        r["skr"][s] = kr_raw[:, 0:QK_ROPE]
        heads = [_softmax_pv(_dot_nt(ks[hd], qs[hd]), None, jnp.concatenate([vts[hd], ones], axis=0), None)
                 for hd in range(HEADS)]
        xs.append(x)
        ycats.append(jnp.concatenate([y_pool] + heads + [y_g], axis=1).astype(BF16))
    out = _ffn_block(jnp.concatenate(xs, axis=0), jnp.concatenate(ycats, axis=0), mod_ref, r,
                     r["g_final"][...] if final else None)
    r["o"][...] = out.reshape(group, seq, D)


def _ctx_layer(x, mod, layer, w, g_final):
    b, seq, _ = x.shape
    final = g_final is not None
    names = ["x", "mod"] + list(MIXER_PARAMS) + list(FFN_PARAMS)
    in_specs = ([pl.BlockSpec((CTX_GROUP, seq, D), lambda bi, i: (bi, 0, 0)), _mod_spec(layer, 0, 0)]
                + _param_specs(MIXER_PARAMS, MIXER_SHAPES, layer)
                + _param_specs(FFN_PARAMS, FFN_SHAPES, layer))
    args = [x, mod] + [w[n] for n in MIXER_PARAMS + FFN_PARAMS]
    if final:
        names.append("g_final")
        in_specs.append(pl.BlockSpec((1, D), lambda bi, i: (0, 0)))
        args.append(g_final)
    names += ["o", "sckv", "skr"]
    return pl.pallas_call(
        functools.partial(_ctx_kernel, names=tuple(names), final=final),
        grid=(b // CTX_GROUP, 1),
        in_specs=in_specs,
        out_specs=[
            pl.BlockSpec((CTX_GROUP, seq, D), lambda bi, i: (bi, 0, 0)),
            pl.BlockSpec((CTX_GROUP, seq, KV_RANK), lambda bi, i: (bi, 0, 0)),
            pl.BlockSpec((CTX_GROUP, seq, QK_ROPE), lambda bi, i: (bi, 0, 0)),
        ],
        out_shape=[
            jax.ShapeDtypeStruct((b, seq, D), F32),
            jax.ShapeDtypeStruct((b, seq, KV_RANK), F32),
            jax.ShapeDtypeStruct((b, seq, QK_ROPE), F32),
        ],
        compiler_params=_params(52),
        name="ctx_layer",
    )(*args)


def _pre_kernel(*refs, names, tm, seq):
    r = dict(zip(names, refs))
    mod_ref = r["mod"]
    i = pl.program_id(1)
    rows = tm + 2 * HALO
    xe = jnp.concatenate([r["x_prev"][0], r["x"][0], r["x_next"][0]], axis=0)
    h = _ada_rms(xe, r["g_mix"][...], mod_ref[:, D:2 * D], mod_ref[:, 0:D]).astype(BF16)
    proj_e = _dot(h, r["w_in"][...])
    proj = proj_e[HALO:HALO + tm]
    proj_g = _dot(h, r["w_in_g"][...])[HALO:HALO + tm]

    pos = i * tm - HALO + jax.lax.broadcasted_iota(jnp.int32, (rows, 1), 0)
    hp_e = jnp.where((pos >= 0) & (pos < seq), proj_e[:, 0:POOL_W], 0.0)
    y_pool = _pool_mix(hp_e, proj[:, 0:POOL_W], i * tm, seq, r)

    rope = (r["rope_c"][...], r["rope_s1"][...], r["rope_s2"][...])
    qs, ks, vts, _, _ = _mla_operands(proj, r, rope)
    for hd in range(HEADS):
        r["q"][0, hd] = qs[hd]
        r["k"][0, hd] = ks[hd]
        r["vt"][0, hd, 0:V_DIM] = vts[hd]
        r["vt"][0, hd, V_DIM:V_AUG] = jnp.ones((V_AUG - V_DIM, tm), BF16)

    y_g = _gmlp(proj_g, r)
    r["ypg"][0] = jnp.concatenate([y_pool, y_g], axis=1).astype(BF16)


def _pre(x, mod, layer, w, rope_tabs, *, tm):
    b, seq, _ = x.shape
    hb = tm // HALO
    nhb = seq // HALO
    names = (["x", "x_prev", "x_next", "mod"] + list(MIXER_PARAMS) + ["rope_c", "rope_s1", "rope_s2"]
             + ["q", "k", "vt", "ypg"])
    in_specs = ([
        pl.BlockSpec((1, tm, D), lambda bi, i: (bi, i, 0)),
        pl.BlockSpec((1, HALO, D), lambda bi, i: (bi, jnp.maximum(i * hb - 1, 0), 0)),
        pl.BlockSpec((1, HALO, D), lambda bi, i: (bi, jnp.minimum((i + 1) * hb, nhb - 1), 0)),
        _mod_spec(layer, 1, 1),
    ] + _param_specs(MIXER_PARAMS, MIXER_SHAPES, layer)
      + [pl.BlockSpec((tm, 128), lambda bi, i: (i, 0))] * 3)
    args = [x, x, x, mod] + [w[n] for n in MIXER_PARAMS] + list(rope_tabs)
    return pl.pallas_call(
        functools.partial(_pre_kernel, names=tuple(names), tm=tm, seq=seq),
        grid=(b, seq // tm),
        in_specs=in_specs,
        out_specs=[
            pl.BlockSpec((1, HEADS, tm, QK_PAD), lambda bi, i: (bi, 0, i, 0)),
            pl.BlockSpec((1, HEADS, tm, QK_PAD), lambda bi, i: (bi, 0, i, 0)),
            pl.BlockSpec((1, HEADS, V_AUG, tm), lambda bi, i: (bi, 0, 0, i)),
            pl.BlockSpec((1, tm, 2 * GMLP_W), lambda bi, i: (bi, i, 0)),
        ],
        out_shape=[
            jax.ShapeDtypeStruct((b, HEADS, seq, QK_PAD), BF16),
            jax.ShapeDtypeStruct((b, HEADS, seq, QK_PAD), BF16),
            jax.ShapeDtypeStruct((b, HEADS, V_AUG, seq), BF16),
            jax.ShapeDtypeStruct((b, seq, 2 * GMLP_W), BF16),
        ],
        compiler_params=_params(48),
        name="pre_lat",
    )(*args)


def _attn_kernel(q_ref, k_ref, vt_ref, kc_ref, vct_ref, o_ref, *, tq):
    units = [(h, slice(u * Q_UNIT, (u + 1) * Q_UNIT)) for h in range(HEADS) for u in range(tq // Q_UNIT)]

    def scores(unit):
        h, rows = unit
        q = q_ref[0, h, rows, :]
        return _dot_nt(k_ref[0, h], q), _dot_nt(kc_ref[0, 0, h], q)

    ahead = [scores(u) for u in units[:LOOKAHEAD]]
    for idx, (h, rows) in enumerate(units):
        st, sct = ahead.pop(0)
        if idx + LOOKAHEAD < len(units):
            ahead.append(scores(units[idx + LOOKAHEAD]))
        o = _softmax_pv(st, sct, vt_ref[0, h], vct_ref[0, 0, h])
        o_ref[0, rows, h * V_DIM:(h + 1) * V_DIM] = o.astype(BF16)


def _attn(q, k, vt, cache, layer, *, tq):
    b, _, seq, _ = q.shape
    kc, vct = cache
    p = kc.shape[3]
    return pl.pallas_call(
        functools.partial(_attn_kernel, tq=tq),
        grid=(b, seq // tq),
        in_specs=[
            pl.BlockSpec((1, HEADS, tq, QK_PAD), lambda bi, i: (bi, 0, i, 0)),
            pl.BlockSpec((1, HEADS, seq, QK_PAD), lambda bi, i: (bi, 0, 0, 0)),
            pl.BlockSpec((1, HEADS, V_AUG, seq), lambda bi, i: (bi, 0, 0, 0)),
            pl.BlockSpec((1, 1, HEADS, p, QK_PAD), lambda bi, i: (bi, layer, 0, 0, 0)),
            pl.BlockSpec((1, 1, HEADS, V_AUG, p), lambda bi, i: (bi, layer, 0, 0, 0)),
        ],
        out_specs=pl.BlockSpec((1, tq, HEADS * V_DIM), lambda bi, i: (bi, i, 0)),
        out_shape=jax.ShapeDtypeStruct((b, seq, HEADS * V_DIM), BF16),
        compiler_params=_params(48),
        name="attn_lat",
    )(q, k, vt, kc, vct)


def _post_kernel(*refs, names, final):
    r = dict(zip(names, refs))
    ypg = r["ypg"][0]
    ycat = jnp.concatenate([ypg[:, 0:POOL_W], r["ymla"][0], ypg[:, POOL_W:]], axis=1)
    r["o"][0] = _ffn_block(r["x"][0], ycat, r["mod"], r, r["g_final"][...] if final else None)


def _post(x, ypg, ymla, mod, layer, w, g_final, *, tm):
    b, seq, _ = x.shape
    final = g_final is not None
    names = ["x", "ypg", "ymla", "mod"] + list(FFN_PARAMS)
    in_specs = [
        pl.BlockSpec((1, tm, D), lambda bi, i: (bi, i, 0)),
        pl.BlockSpec((1, tm, 2 * GMLP_W), lambda bi, i: (bi, i, 0)),
        pl.BlockSpec((1, tm, HEADS * V_DIM), lambda bi, i: (bi, i, 0)),
        _mod_spec(layer, 1, 1),
    ] + _param_specs(FFN_PARAMS, FFN_SHAPES, layer)
    args = [x, ypg, ymla, mod] + [w[n] for n in FFN_PARAMS]
    if final:
        names.append("g_final")
        in_specs.append(pl.BlockSpec((1, D), lambda bi, i: (0, 0)))
        args.append(g_final)
    names.append("o")
    return pl.pallas_call(
        functools.partial(_post_kernel, names=tuple(names), final=final),
        grid=(b, seq // tm),
        in_specs=in_specs,
        out_specs=pl.BlockSpec((1, tm, D), lambda bi, i: (bi, i, 0)),
        out_shape=jax.ShapeDtypeStruct((b, seq, D), F32),
        compiler_params=_params(56),
        name="post_lat",
    )(*args)


def _rope_tables(seq):
    rows = seq // GRID_W
    row = jnp.repeat(jnp.arange(rows, dtype=F32), GRID_W)
    col = jnp.tile(jnp.arange(GRID_W, dtype=F32), rows)
    n_freq = QK_ROPE // 4
    inv = 1.0 / (ROPE_THETA ** (jnp.arange(n_freq, dtype=F32) / n_freq))
    ang = jnp.concatenate([row[:, None] * inv, col[:, None] * inv], axis=-1)
    cos, sin = jnp.cos(ang), jnp.sin(ang)
    z = jnp.zeros_like(cos)
    c = jnp.concatenate([cos, cos, z, z], axis=-1)
    s1 = jnp.concatenate([-sin, z, z, z], axis=-1)
    s2 = jnp.concatenate([z, sin, z, z], axis=-1)
    return c, s1, s2


def kernel(x_prompt, x_sample, cache_ckv, cache_krope, c, c_ctx, w_ada, b_ada, g_mix, w_in, w_pool,
           pool_scale, g_q, w_uq, g_kv, w_ukv, g_sgu, w_s, b_s, w_out, g_ffn, w_ff1, w_ff2, g_final):
    dec_b = x_sample.shape[0]
    w_in_p = jnp.pad(w_in[:, :, :P_R + QK_ROPE].astype(BF16), ((0, 0), (0, 0), (0, 128 - QK_ROPE)))
    w_in_g = w_in[:, :, P_R + QK_ROPE:].astype(BF16)
    w_uq_h = w_uq.astype(BF16).reshape(DEPTH, Q_RANK, HEADS, QK_NOPE + QK_ROPE)
    w_uq_p = jnp.pad(w_uq_h, ((0, 0), (0, 0), (0, 0), (0, QK_PAD - QK_NOPE - QK_ROPE)))
    w_uq_p = w_uq_p.reshape(DEPTH, Q_RANK, HEADS * QK_PAD)
    w_ukv_h = w_ukv.astype(BF16).reshape(DEPTH, KV_RANK, HEADS, QK_NOPE + V_DIM)
    w_uk = w_ukv_h[..., :QK_NOPE].reshape(DEPTH, KV_RANK, HEADS * QK_NOPE)
    w_uvt = jnp.swapaxes(w_ukv_h[..., QK_NOPE:].reshape(DEPTH, KV_RANK, HEADS * V_DIM), 1, 2)
    w_pool_bd = jnp.zeros((DEPTH, POOL_W, POOL_W), BF16)
    for g in range(len(POOL_WINDOWS)):
        sl = slice(g * POOL_GD, (g + 1) * POOL_GD)
        w_pool_bd = w_pool_bd.at[:, sl, sl].set(w_pool[:, g].astype(BF16))
    weights = {
        "g_mix": g_mix.reshape(DEPTH, 1, D),
        "w_in": w_in_p,
        "w_in_g": w_in_g,
        "g_q": (g_q * SM_SCALE).reshape(DEPTH, 1, Q_RANK),
        "w_uq": w_uq_p,
        "g_kv": g_kv.reshape(DEPTH, 1, KV_RANK),
        "w_uk": w_uk,
        "w_uvt": w_uvt,
        "w_pool": w_pool_bd,
        "pool_scale": pool_scale.reshape(DEPTH, 1, POOL_W),
        "g_sgu": g_sgu.reshape(DEPTH, 1, GMLP_W),
        "w_s": w_s.reshape(DEPTH, GMLP_G * CHUNK, CHUNK).astype(BF16),
        "b_s": jnp.repeat(jnp.swapaxes(b_s, 1, 2), GMLP_GD, axis=2),
        "w_out": w_out.astype(BF16),
        "g_ffn": g_ffn.reshape(DEPTH, 1, D),
        "w_ff1": w_ff1.astype(BF16),
        "w_ff2": w_ff2.astype(BF16),
    }
    g_fin = g_final.reshape(1, D)
    rope_tabs = _rope_tables(x_sample.shape[1])

    n_rows = 16
    c_all = jnp.zeros((n_rows, D), F32).at[0].set(c_ctx).at[1:1 + dec_b].set(c)
    mod = _ada(c_all, w_ada, b_ada).reshape(DEPTH, n_rows, 1, 6 * D)

    cache_kr_pad = jnp.pad(cache_krope, ((0, 0), (0, 0), (0, 0), (0, 128 - QK_ROPE))).astype(BF16)
    cache = _cache_expand(cache_ckv, cache_kr_pad, w_uk, w_uvt)

    xp, xs = x_prompt, x_sample
    ckv_list, kr_list = [], []
    for l in range(DEPTH):
        g_last = g_fin if l == DEPTH - 1 else None
        xp, sckv, skr = _ctx_layer(xp, mod, l, weights, g_last)
        ckv_list.append(sckv)
        kr_list.append(skr)
        q, k, vt, ypg = _pre(xs, mod, l, weights, rope_tabs, tm=512)
        ymla = _attn(q, k, vt, cache, l, tq=1024)
        xs = _post(xs, ypg, ymla, mod, l, weights, g_last, tm=512)
    return xp, xs, jnp.stack(ckv_list, axis=1), jnp.stack(kr_list, axis=1)
```

```python
import functools
import math

import jax
import jax.numpy as jnp
from jax.experimental import pallas as pl
from jax.experimental.pallas import tpu as pltpu

D = 1024
DEPTH = 4
GRID_W = 64
EPS = 1e-6
POOL_W = 256
POOL_WINDOWS = (2, 4, 8, 16)
POOL_GD = 64
HALO = 8
QK_NOPE = 128
QK_ROPE = 64
V_DIM = 128
V_AUG = V_DIM + 16
HEADS = 4
Q_RANK = 384
KV_RANK = 256
ROPE_THETA = 10000.0
CHUNK = 128
GMLP_W = 256
GMLP_G = 4
GMLP_GD = 64
D_FF = 4096
FF_CHUNK = 1024
P_Q = 256
P_KV = P_Q + Q_RANK
P_R = P_KV + KV_RANK
P_G = P_R + 128
QK_PAD = 256
Q_PAIR = 2 * QK_NOPE + 2 * QK_ROPE
Q_UNIT = 256
CTX_GROUP = 2
LOOKAHEAD = 2
SM_SCALE = math.log2(math.e) / math.sqrt(QK_NOPE + QK_ROPE)
NT_DIMS = (((1,), (1,)), ((), ()))

F32 = jnp.float32
BF16 = jnp.bfloat16
MIB = 1024 * 1024

MIXER_PARAMS = ("g_mix", "w_in", "w_in_g", "g_q", "w_uq", "g_kv", "w_uk", "w_uvt",
                "w_pool", "pool_scale", "g_sgu", "w_s", "b_s")
MIXER_SHAPES = {
    "g_mix": (1, D), "w_in": (D, P_G), "w_in_g": (D, 2 * GMLP_W), "g_q": (1, Q_RANK),
    "w_uq": (Q_RANK, HEADS // 2 * Q_PAIR), "g_kv": (1, KV_RANK), "w_uk": (KV_RANK, HEADS * QK_NOPE),
    "w_uvt": (HEADS * V_DIM, KV_RANK), "w_pool": (POOL_W, POOL_W), "pool_scale": (1, POOL_W),
    "g_sgu": (1, GMLP_W), "w_s": (GMLP_G * CHUNK, CHUNK), "b_s": (CHUNK, GMLP_W),
}
FFN_PARAMS = ("w_out", "g_ffn", "w_ff1", "w_ff2")
FFN_SHAPES = {"w_out": (D, D), "g_ffn": (1, D), "w_ff1": (D, D_FF), "w_ff2": (D_FF, D)}
SINGLE_BUFFERED = ("w_in", "w_out", "w_ff1", "w_ff2")


def _params(vmem_mib):
    return pltpu.CompilerParams(
        dimension_semantics=("arbitrary", "arbitrary"),
        vmem_limit_bytes=vmem_mib * MIB,
    )


def _rms(x, g):
    y = x * jax.lax.rsqrt(jnp.mean(x * x, axis=-1, keepdims=True) + EPS)
    return y * g


def _ada_rms(x, g, scale, shift):
    y = x * jax.lax.rsqrt(jnp.mean(x * x, axis=-1, keepdims=True) + EPS)
    return y * (g * (1.0 + scale)) + shift


def _dot(a, b):
    return jnp.dot(a, b, preferred_element_type=F32)


def _dot_nt(a, b):
    return jax.lax.dot_general(a, b, NT_DIMS, preferred_element_type=F32)


def _layer_spec(shape, layer, single=False):
    index_map = lambda bi, i: (layer,) + (0,) * len(shape)
    if single:
        return pl.BlockSpec((None,) + shape, index_map, pipeline_mode=pl.Buffered(1))
    return pl.BlockSpec((None,) + shape, index_map)


def _param_specs(names, shapes, layer):
    return [_layer_spec(shapes[n], layer, n in SINGLE_BUFFERED) for n in names]


def _mod_spec(layer, row0, per_batch):
    return pl.BlockSpec((None, None, 1, 6 * D), lambda bi, i: (layer, bi * per_batch + row0, 0, 0))


def _ada_kernel(c_ref, w_ref, b_ref, o_ref):
    c = c_ref[...]
    s = c / (1.0 + jnp.exp(-c))
    o_ref[0] = _dot(s.astype(BF16), w_ref[0].astype(BF16)) + b_ref[0]


def _ada(c_all, w_ada, b_ada):
    rows = c_all.shape[0]
    tn = 1536
    return pl.pallas_call(
        _ada_kernel,
        grid=(DEPTH, 6 * D // tn),
        in_specs=[
            pl.BlockSpec((rows, D), lambda l, j: (0, 0)),
            pl.BlockSpec((1, D, tn), lambda l, j: (l, 0, j)),
            pl.BlockSpec((1, 1, tn), lambda l, j: (l, 0, j)),
        ],
        out_specs=pl.BlockSpec((1, rows, tn), lambda l, j: (l, 0, j)),
        out_shape=jax.ShapeDtypeStruct((DEPTH, rows, 6 * D), F32),
        compiler_params=_params(32),
        name="ada",
    )(c_all, w_ada, b_ada.reshape(DEPTH, 1, 6 * D))


def _win_prep_kernel(w_ref, wa_ref, wg_ref):
    lane = jax.lax.broadcasted_iota(jnp.int32, (1, P_G - P_R), 1)
    wa_ref[:, 0:P_R] = w_ref[:, 0:P_R].astype(BF16)
    wa_ref[:, P_R:P_G] = jnp.where(lane < QK_ROPE, w_ref[:, P_R:P_G], 0.0).astype(BF16)
    wg_ref[...] = w_ref[:, P_R + QK_ROPE:].astype(BF16)


def _win_prep(w_in):
    cols = w_in.shape[2]
    return pl.pallas_call(
        _win_prep_kernel,
        grid=(DEPTH,),
        in_specs=[pl.BlockSpec((None, D, cols), lambda l: (l, 0, 0))],
        out_specs=[
            pl.BlockSpec((None, D, P_G), lambda l: (l, 0, 0)),
            pl.BlockSpec((None, D, 2 * GMLP_W), lambda l: (l, 0, 0)),
        ],
        out_shape=[
            jax.ShapeDtypeStruct((DEPTH, D, P_G), BF16),
            jax.ShapeDtypeStruct((DEPTH, D, 2 * GMLP_W), BF16),
        ],
        compiler_params=pltpu.CompilerParams(dimension_semantics=("arbitrary",), vmem_limit_bytes=32 * MIB),
        name="win_prep",
    )(w_in)


def _cache_kernel(ckv_ref, kr_ref, wk_ref, wvt_ref, k_ref, vt_ref):
    for s in range(ckv_ref.shape[0]):
        ckv = ckv_ref[s, 0].astype(BF16)
        kn = _dot(ckv, wk_ref[...])
        vt = _dot_nt(wvt_ref[...], ckv)
        kr = kr_ref[s, 0]
        kr_tiles = (kr.astype(BF16), pltpu.roll(kr, QK_ROPE, axis=1).astype(BF16))
        for h in range(HEADS):
            k_ref[s, 0, h] = jnp.concatenate(
                [kn[:, h * QK_NOPE:(h + 1) * QK_NOPE].astype(BF16), kr_tiles[h % 2]], axis=1)
            vt_ref[s, 0, h, 0:V_DIM] = vt[h * V_DIM:(h + 1) * V_DIM].astype(BF16)
            vt_ref[s, 0, h, V_DIM:V_AUG] = jnp.ones((V_AUG - V_DIM, vt.shape[1]), BF16)


def _cache_expand(cache_ckv, cache_kr_pad, w_uk, w_uvt):
    b, depth, p, _ = cache_ckv.shape
    group = 4
    return pl.pallas_call(
        _cache_kernel,
        grid=(depth, b // group),
        in_specs=[
            pl.BlockSpec((group, 1, p, KV_RANK), lambda l, i: (i, l, 0, 0)),
            pl.BlockSpec((group, 1, p, 128), lambda l, i: (i, l, 0, 0)),
            pl.BlockSpec((None, KV_RANK, HEADS * QK_NOPE), lambda l, i: (l, 0, 0)),
            pl.BlockSpec((None, HEADS * V_DIM, KV_RANK), lambda l, i: (l, 0, 0)),
        ],
        out_specs=[
            pl.BlockSpec((group, 1, HEADS, p, QK_PAD), lambda l, i: (i, l, 0, 0, 0)),
            pl.BlockSpec((group, 1, HEADS, V_AUG, p), lambda l, i: (i, l, 0, 0, 0)),
        ],
        out_shape=[
            jax.ShapeDtypeStruct((b, depth, HEADS, p, QK_PAD), BF16),
            jax.ShapeDtypeStruct((b, depth, HEADS, V_AUG, p), BF16),
        ],
        compiler_params=_params(32),
        name="cache_expand",
    )(cache_ckv, cache_kr_pad, w_uk, w_uvt)


def _rows_up(a, k):
    return pltpu.roll(a, a.shape[0] - k, axis=0)


def _rows_down(a, k):
    return pltpu.roll(a, k, axis=0)


def _rope(x, c, s1, s2):
    return x * c + pltpu.roll(x, 96, axis=1) * s1 + pltpu.roll(x, 32, axis=1) * s2


def _pool_mix(hp_e, hp, t0, seq, r):
    tm = hp.shape[0]
    main = slice(HALO, HALO + tm)
    lane_grp = jax.lax.broadcasted_iota(jnp.int32, (1, 2 * POOL_GD), 1) // POOL_GD
    xa = hp_e[:, 0:2 * POOL_GD]
    s2 = xa + _rows_down(xa, 1)
    s4 = _rows_up(s2, 1) + _rows_down(s2, 1)
    win_a = jnp.where(lane_grp == 0, s2[main], s4[main])
    xb = hp_e[:, 2 * POOL_GD:POOL_W]
    f2 = xb + _rows_up(xb, 1)
    f4 = f2 + _rows_up(f2, 2)
    f8 = f4 + _rows_up(f4, 4)
    s8 = _rows_down(f8, 4)
    s16 = f8[0:tm] + f8[main]
    win_b = jnp.where(lane_grp == 0, s8[main], s16)
    win = jnp.concatenate([win_a, win_b], axis=1)
    grp = jax.lax.broadcasted_iota(jnp.int32, (1, POOL_W), 1) // POOL_GD
    half = jnp.where(grp == 0, 1, jnp.where(grp == 1, 2, jnp.where(grp == 2, 4, 8)))
    t = t0 + jax.lax.broadcasted_iota(jnp.int32, (tm, 1), 0)
    cnt = jnp.clip(t + half, 0, seq) - jnp.clip(t - half, 0, seq)
    pooled = win / cnt.astype(F32) - hp
    return _dot(pooled.astype(BF16), r["w_pool"][...]) * r["pool_scale"][...]


def _gmlp(proj_g, r):
    tm = proj_g.shape[0]
    uv = jax.nn.gelu(proj_g)
    u = uv[:, 0:GMLP_W]
    vg = _rms(uv[:, GMLP_W:], r["g_sgu"][...]).astype(BF16)
    ggrp = jax.lax.broadcasted_iota(jnp.int32, (1, GMLP_W), 1) // GMLP_GD
    ys = []
    for cix in range(tm // CHUNK):
        m = _dot(r["w_s"][...], vg[cix * CHUNK:(cix + 1) * CHUNK])
        mixed = jnp.where(
            ggrp == 0, m[0:CHUNK],
            jnp.where(ggrp == 1, m[CHUNK:2 * CHUNK],
                      jnp.where(ggrp == 2, m[2 * CHUNK:3 * CHUNK], m[3 * CHUNK:4 * CHUNK])))
        ys.append(u[cix * CHUNK:(cix + 1) * CHUNK] * (mixed + r["b_s"][...]))
    return jnp.concatenate(ys, axis=0)


def _mla_operands(proj, r, rope):
    cq = _rms(proj[:, P_Q:P_KV], r["g_q"][...])
    q = _dot(cq.astype(BF16), r["w_uq"][...])
    ckv = _rms(proj[:, P_KV:P_R], r["g_kv"][...])
    ckv_b = ckv.astype(BF16)
    kn = _dot(ckv_b, r["w_uk"][...])
    vt = _dot_nt(r["w_uvt"][...], ckv_b)
    kr_raw = proj[:, P_R:P_G]
    kr = kr_raw if rope is None else _rope(kr_raw, *rope)
    kr_tiles = (kr.astype(BF16), pltpu.roll(kr, QK_ROPE, axis=1).astype(BF16))
    qs, ks, vts = [], [], []
    for pair in range(HEADS // 2):
        base = pair * Q_PAIR
        qr = q[:, base + 2 * QK_NOPE:base + Q_PAIR]
        if rope is not None:
            qr = _rope(qr, *rope)
        qr_b = qr.astype(BF16)
        for j in range(2):
            hd = 2 * pair + j
            qn = q[:, base + j * QK_NOPE:base + (j + 1) * QK_NOPE]
            qs.append(jnp.concatenate([qn.astype(BF16), qr_b], axis=1))
            ks.append(jnp.concatenate([kn[:, hd * QK_NOPE:(hd + 1) * QK_NOPE].astype(BF16), kr_tiles[j]], axis=1))
            vts.append(vt[hd * V_DIM:(hd + 1) * V_DIM].astype(BF16))
    return qs, ks, vts, ckv, kr_raw


def _softmax_pv(st, sct, vt, vct):
    m = jnp.max(st, axis=0, keepdims=True)
    if sct is not None:
        m = jnp.maximum(m, jnp.max(sct, axis=0, keepdims=True))
    ot = _dot(vt, jnp.exp2(st - m).astype(BF16))
    if sct is not None:
        ot = ot + _dot(vct, jnp.exp2(sct - m).astype(BF16))
    return (ot[0:V_DIM] / ot[V_DIM:V_DIM + 1]).T


def _ffn_block(x, ycat, mod_ref, r, g_final):
    g1 = mod_ref[:, 2 * D:3 * D]
    sh2 = mod_ref[:, 3 * D:4 * D]
    sc2 = mod_ref[:, 4 * D:5 * D]
    g2 = mod_ref[:, 5 * D:6 * D]
    x1 = x + g1 * _dot(ycat, r["w_out"][...])
    h = _ada_rms(x1, r["g_ffn"][...], sc2, sh2).astype(BF16)
    acc = None
    for j in range(D_FF // FF_CHUNK):
        a = _dot(h, r["w_ff1"][:, j * FF_CHUNK:(j + 1) * FF_CHUNK])
        a = jnp.square(jnp.maximum(a, 0.0)).astype(BF16)
        part = _dot(a, r["w_ff2"][j * FF_CHUNK:(j + 1) * FF_CHUNK, :])
        acc = part if acc is None else acc + part
    out = x1 + g2 * acc
    if g_final is not None:
        out = _rms(out, g_final)
    return out


def _ctx_kernel(*refs, names, final):
    r = dict(zip(names, refs))
    mod_ref = r["mod"]
    group, seq, _ = r["x"].shape
    edge = jnp.zeros((HALO, POOL_W), F32)
    ones = jnp.ones((V_AUG - V_DIM, seq), BF16)
    xs, ycats = [], []
    for s in range(group):
        x = r["x"][s]
        h = _ada_rms(x, r["g_mix"][...], mod_ref[:, D:2 * D], mod_ref[:, 0:D]).astype(BF16)
        proj = _dot(h, r["w_in"][...])
        proj_g = _dot(h, r["w_in_g"][...])
        hp = proj[:, 0:POOL_W]
        y_pool = _pool_mix(jnp.concatenate([edge, hp, edge], axis=0), hp, 0, seq, r)
        y_g = _gmlp(proj_g, r)
        qs, ks, vts, ckv, kr_raw = _mla_operands(proj, r, None)
        r["sckv"][s] = ckv
        r["skr"][s] = kr_raw[:, 0:QK_ROPE]
        heads = [_softmax_pv(_dot_nt(ks[hd], qs[hd]), None, jnp.concatenate([vts[hd], ones], axis=0), None)
                 for hd in range(HEADS)]
        xs.append(x)
        ycats.append(jnp.concatenate([y_pool] + heads + [y_g], axis=1).astype(BF16))
    out = _ffn_block(jnp.concatenate(xs, axis=0), jnp.concatenate(ycats, axis=0), mod_ref, r,
                     r["g_final"][...] if final else None)
    r["o"][...] = out.reshape(group, seq, D)


def _ctx_layer(x, mod, layer, w, g_final):
    b, seq, _ = x.shape
    final = g_final is not None
    names = ["x", "mod"] + list(MIXER_PARAMS) + list(FFN_PARAMS)
    in_specs = ([pl.BlockSpec((CTX_GROUP, seq, D), lambda bi, i: (bi, 0, 0)), _mod_spec(layer, 0, 0)]
                + _param_specs(MIXER_PARAMS, MIXER_SHAPES, layer)
                + _param_specs(FFN_PARAMS, FFN_SHAPES, layer))
    args = [x, mod] + [w[n] for n in MIXER_PARAMS + FFN_PARAMS]
    if final:
        names.append("g_final")
        in_specs.append(pl.BlockSpec((1, D), lambda bi, i: (0, 0)))
        args.append(g_final)
    names += ["o", "sckv", "skr"]
    return pl.pallas_call(
        functools.partial(_ctx_kernel, names=tuple(names), final=final),
        grid=(b // CTX_GROUP, 1),
        in_specs=in_specs,
        out_specs=[
            pl.BlockSpec((CTX_GROUP, seq, D), lambda bi, i: (bi, 0, 0)),
            pl.BlockSpec((CTX_GROUP, seq, KV_RANK), lambda bi, i: (bi, 0, 0)),
            pl.BlockSpec((CTX_GROUP, seq, QK_ROPE), lambda bi, i: (bi, 0, 0)),
        ],
        out_shape=[
            jax.ShapeDtypeStruct((b, seq, D), F32),
            jax.ShapeDtypeStruct((b, seq, KV_RANK), F32),
            jax.ShapeDtypeStruct((b, seq, QK_ROPE), F32),
        ],
        compiler_params=_params(52),
        name="ctx_layer",
    )(*args)


def _pre_kernel(*refs, names, tm, seq):
    r = dict(zip(names, refs))
    mod_ref = r["mod"]
    i = pl.program_id(1)
    rows = tm + 2 * HALO
    xe = jnp.concatenate([r["x_prev"][0], r["x"][0], r["x_next"][0]], axis=0)
    h = _ada_rms(xe, r["g_mix"][...], mod_ref[:, D:2 * D], mod_ref[:, 0:D]).astype(BF16)
    proj_e = _dot(h, r["w_in"][...])
    proj = proj_e[HALO:HALO + tm]
    proj_g = _dot(h, r["w_in_g"][...])[HALO:HALO + tm]

    pos = i * tm - HALO + jax.lax.broadcasted_iota(jnp.int32, (rows, 1), 0)
    hp_e = jnp.where((pos >= 0) & (pos < seq), proj_e[:, 0:POOL_W], 0.0)
    y_pool = _pool_mix(hp_e, proj[:, 0:POOL_W], i * tm, seq, r)

    rope = (r["rope_c"][...], r["rope_s1"][...], r["rope_s2"][...])
    qs, ks, vts, _, _ = _mla_operands(proj, r, rope)
    for hd in range(HEADS):
        r["q"][0, hd] = qs[hd]
        r["k"][0, hd] = ks[hd]
        r["vt"][0, hd, 0:V_DIM] = vts[hd]
        r["vt"][0, hd, V_DIM:V_AUG] = jnp.ones((V_AUG - V_DIM, tm), BF16)

    y_g = _gmlp(proj_g, r)
    r["ypg"][0] = jnp.concatenate([y_pool, y_g], axis=1).astype(BF16)


def _pre(x, mod, layer, w, rope_tabs, *, tm):
    b, seq, _ = x.shape
    hb = tm // HALO
    nhb = seq // HALO
    names = (["x", "x_prev", "x_next", "mod"] + list(MIXER_PARAMS) + ["rope_c", "rope_s1", "rope_s2"]
             + ["q", "k", "vt", "ypg"])
    in_specs = ([
        pl.BlockSpec((1, tm, D), lambda bi, i: (bi, i, 0)),
        pl.BlockSpec((1, HALO, D), lambda bi, i: (bi, jnp.maximum(i * hb - 1, 0), 0)),
        pl.BlockSpec((1, HALO, D), lambda bi, i: (bi, jnp.minimum((i + 1) * hb, nhb - 1), 0)),
        _mod_spec(layer, 1, 1),
    ] + _param_specs(MIXER_PARAMS, MIXER_SHAPES, layer)
      + [pl.BlockSpec((tm, 128), lambda bi, i: (i, 0))] * 3)
    args = [x, x, x, mod] + [w[n] for n in MIXER_PARAMS] + list(rope_tabs)
    return pl.pallas_call(
        functools.partial(_pre_kernel, names=tuple(names), tm=tm, seq=seq),
        grid=(b, seq // tm),
        in_specs=in_specs,
        out_specs=[
            pl.BlockSpec((1, HEADS, tm, QK_PAD), lambda bi, i: (bi, 0, i, 0)),
            pl.BlockSpec((1, HEADS, tm, QK_PAD), lambda bi, i: (bi, 0, i, 0)),
            pl.BlockSpec((1, HEADS, V_AUG, tm), lambda bi, i: (bi, 0, 0, i)),
            pl.BlockSpec((1, tm, 2 * GMLP_W), lambda bi, i: (bi, i, 0)),
        ],
        out_shape=[
            jax.ShapeDtypeStruct((b, HEADS, seq, QK_PAD), BF16),
            jax.ShapeDtypeStruct((b, HEADS, seq, QK_PAD), BF16),
            jax.ShapeDtypeStruct((b, HEADS, V_AUG, seq), BF16),
            jax.ShapeDtypeStruct((b, seq, 2 * GMLP_W), BF16),
        ],
        compiler_params=_params(48),
        name="pre_lat",
    )(*args)


def _attn_kernel(q_ref, k_ref, vt_ref, kc_ref, vct_ref, o_ref, *, tq):
    units = [(h, slice(u * Q_UNIT, (u + 1) * Q_UNIT)) for h in range(HEADS) for u in range(tq // Q_UNIT)]

    def scores(unit):
        h, rows = unit
        q = q_ref[0, h, rows, :]
        return _dot_nt(k_ref[0, h], q), _dot_nt(kc_ref[0, 0, h], q)

    ahead = [scores(u) for u in units[:LOOKAHEAD]]
    for idx, (h, rows) in enumerate(units):
        st, sct = ahead.pop(0)
        if idx + LOOKAHEAD < len(units):
            ahead.append(scores(units[idx + LOOKAHEAD]))
        o = _softmax_pv(st, sct, vt_ref[0, h], vct_ref[0, 0, h])
        o_ref[0, rows, h * V_DIM:(h + 1) * V_DIM] = o.astype(BF16)


def _attn(q, k, vt, cache, layer, *, tq):
    b, _, seq, _ = q.shape
    kc, vct = cache
    p = kc.shape[3]
    return pl.pallas_call(
        functools.partial(_attn_kernel, tq=tq),
        grid=(b, seq // tq),
        in_specs=[
            pl.BlockSpec((1, HEADS, tq, QK_PAD), lambda bi, i: (bi, 0, i, 0)),
            pl.BlockSpec((1, HEADS, seq, QK_PAD), lambda bi, i: (bi, 0, 0, 0)),
            pl.BlockSpec((1, HEADS, V_AUG, seq), lambda bi, i: (bi, 0, 0, 0)),
            pl.BlockSpec((1, 1, HEADS, p, QK_PAD), lambda bi, i: (bi, layer, 0, 0, 0)),
            pl.BlockSpec((1, 1, HEADS, V_AUG, p), lambda bi, i: (bi, layer, 0, 0, 0)),
        ],
        out_specs=pl.BlockSpec((1, tq, HEADS * V_DIM), lambda bi, i: (bi, i, 0)),
        out_shape=jax.ShapeDtypeStruct((b, seq, HEADS * V_DIM), BF16),
        compiler_params=_params(48),
        name="attn_lat",
    )(q, k, vt, kc, vct)


def _post_kernel(*refs, names, final):
    r = dict(zip(names, refs))
    ypg = r["ypg"][0]
    ycat = jnp.concatenate([ypg[:, 0:POOL_W], r["ymla"][0], ypg[:, POOL_W:]], axis=1)
    r["o"][0] = _ffn_block(r["x"][0], ycat, r["mod"], r, r["g_final"][...] if final else None)


def _post(x, ypg, ymla, mod, layer, w, g_final, *, tm):
    b, seq, _ = x.shape
    final = g_final is not None
    names = ["x", "ypg", "ymla", "mod"] + list(FFN_PARAMS)
    in_specs = [
        pl.BlockSpec((1, tm, D), lambda bi, i: (bi, i, 0)),
        pl.BlockSpec((1, tm, 2 * GMLP_W), lambda bi, i: (bi, i, 0)),
        pl.BlockSpec((1, tm, HEADS * V_DIM), lambda bi, i: (bi, i, 0)),
        _mod_spec(layer, 1, 1),
    ] + _param_specs(FFN_PARAMS, FFN_SHAPES, layer)
    args = [x, ypg, ymla, mod] + [w[n] for n in FFN_PARAMS]
    if final:
        names.append("g_final")
        in_specs.append(pl.BlockSpec((1, D), lambda bi, i: (0, 0)))
        args.append(g_final)
    names.append("o")
    return pl.pallas_call(
        functools.partial(_post_kernel, names=tuple(names), final=final),
        grid=(b, seq // tm),
        in_specs=in_specs,
        out_specs=pl.BlockSpec((1, tm, D), lambda bi, i: (bi, i, 0)),
        out_shape=jax.ShapeDtypeStruct((b, seq, D), F32),
        compiler_params=_params(56),
        name="post_lat",
    )(*args)


def _rope_tables(seq):
    rows = seq // GRID_W
    row = jnp.repeat(jnp.arange(rows, dtype=F32), GRID_W)
    col = jnp.tile(jnp.arange(GRID_W, dtype=F32), rows)
    n_freq = QK_ROPE // 4
    inv = 1.0 / (ROPE_THETA ** (jnp.arange(n_freq, dtype=F32) / n_freq))
    ang = jnp.concatenate([row[:, None] * inv, col[:, None] * inv], axis=-1)
    cos, sin = jnp.cos(ang), jnp.sin(ang)
    z = jnp.zeros_like(cos)
    c = jnp.concatenate([cos, cos, cos, cos], axis=-1)
    s1 = jnp.concatenate([-sin, z, -sin, z], axis=-1)
    s2 = jnp.concatenate([z, sin, z, sin], axis=-1)
    return c, s1, s2


def kernel(x_prompt, x_sample, cache_ckv, cache_krope, c, c_ctx, w_ada, b_ada, g_mix, w_in, w_pool,
           pool_scale, g_q, w_uq, g_kv, w_ukv, g_sgu, w_s, b_s, w_out, g_ffn, w_ff1, w_ff2, g_final):
    dec_b = x_sample.shape[0]
    w_in_p, w_in_g = _win_prep(w_in)
    w_uq_h = w_uq.astype(BF16).reshape(DEPTH, Q_RANK, HEADS // 2, 2, QK_NOPE + QK_ROPE)
    w_uq_p = jnp.concatenate([w_uq_h[..., 0, :QK_NOPE], w_uq_h[..., 1, :QK_NOPE],
                              w_uq_h[..., 0, QK_NOPE:], w_uq_h[..., 1, QK_NOPE:]], axis=-1)
    w_uq_p = w_uq_p.reshape(DEPTH, Q_RANK, HEADS // 2 * Q_PAIR)
    w_ukv_h = w_ukv.astype(BF16).reshape(DEPTH, KV_RANK, HEADS, QK_NOPE + V_DIM)
    w_uk = w_ukv_h[..., :QK_NOPE].reshape(DEPTH, KV_RANK, HEADS * QK_NOPE)
    w_uvt = jnp.swapaxes(w_ukv_h[..., QK_NOPE:].reshape(DEPTH, KV_RANK, HEADS * V_DIM), 1, 2)
    w_pool_bd = jnp.zeros((DEPTH, POOL_W, POOL_W), BF16)
    for g in range(len(POOL_WINDOWS)):
        sl = slice(g * POOL_GD, (g + 1) * POOL_GD)
        w_pool_bd = w_pool_bd.at[:, sl, sl].set(w_pool[:, g].astype(BF16))
    weights = {
        "g_mix": g_mix.reshape(DEPTH, 1, D),
        "w_in": w_in_p,
        "w_in_g": w_in_g,
        "g_q": (g_q * SM_SCALE).reshape(DEPTH, 1, Q_RANK),
        "w_uq": w_uq_p,
        "g_kv": g_kv.reshape(DEPTH, 1, KV_RANK),
        "w_uk": w_uk,
        "w_uvt": w_uvt,
        "w_pool": w_pool_bd,
        "pool_scale": pool_scale.reshape(DEPTH, 1, POOL_W),
        "g_sgu": g_sgu.reshape(DEPTH, 1, GMLP_W),
        "w_s": w_s.reshape(DEPTH, GMLP_G * CHUNK, CHUNK).astype(BF16),
        "b_s": jnp.repeat(jnp.swapaxes(b_s, 1, 2), GMLP_GD, axis=2),
        "w_out": w_out.astype(BF16),
        "g_ffn": g_ffn.reshape(DEPTH, 1, D),
        "w_ff1": w_ff1.astype(BF16),
        "w_ff2": w_ff2.astype(BF16),
    }
    g_fin = g_final.reshape(1, D)
    rope_tabs = _rope_tables(x_sample.shape[1])

    n_rows = 16
    c_all = jnp.zeros((n_rows, D), F32).at[0].set(c_ctx).at[1:1 + dec_b].set(c)
    mod = _ada(c_all, w_ada, b_ada).reshape(DEPTH, n_rows, 1, 6 * D)

    cache_kr_pad = jnp.pad(cache_krope, ((0, 0), (0, 0), (0, 0), (0, 128 - QK_ROPE)))
    cache = _cache_expand(cache_ckv, cache_kr_pad, w_uk, w_uvt)

    xp, xs = x_prompt, x_sample
    ckv_list, kr_list = [], []
    for l in range(DEPTH):
        g_last = g_fin if l == DEPTH - 1 else None
        xp, sckv, skr = _ctx_layer(xp, mod, l, weights, g_last)
        ckv_list.append(sckv)
        kr_list.append(skr)
        q, k, vt, ypg = _pre(xs, mod, l, weights, rope_tabs, tm=512)
        ymla = _attn(q, k, vt, cache, l, tq=1024)
        xs = _post(xs, ypg, ymla, mod, l, weights, g_last, tm=512)
    return xp, xs, jnp.stack(ckv_list, axis=1), jnp.stack(kr_list, axis=1)
```

```python
import functools
import math

import jax
import jax.numpy as jnp
from jax.experimental import pallas as pl
from jax.experimental.pallas import tpu as pltpu

D = 1024
DEPTH = 4
GRID_W = 64
EPS = 1e-6
POOL_W = 256
POOL_WINDOWS = (2, 4, 8, 16)
POOL_GD = 64
HALO = 8
QK_NOPE = 128
QK_ROPE = 64
V_DIM = 128
V_PAD = 2 * V_DIM
HEADS = 4
Q_RANK = 384
KV_RANK = 256
ROPE_THETA = 10000.0
CHUNK = 128
GMLP_W = 256
GMLP_G = 4
GMLP_GD = 64
D_FF = 4096
FF_CHUNK = 1024
P_Q = 256
P_KV = P_Q + Q_RANK
P_R = P_KV + KV_RANK
P_G = P_R + 128
QK_PAD = 256
Q_PAIR = 2 * QK_NOPE + 2 * QK_ROPE
Q_UNIT = 512
CTX_GROUP = 2
LOOKAHEAD = 2
SM_SCALE = math.log2(math.e) / math.sqrt(QK_NOPE + QK_ROPE)
NT_DIMS = (((1,), (1,)), ((), ()))

F32 = jnp.float32
BF16 = jnp.bfloat16
MIB = 1024 * 1024

MIXER_PARAMS = ("g_mix", "w_in", "w_in_g", "g_q", "w_uq", "g_kv", "w_uk", "w_uv",
                "w_pool", "pool_scale", "g_sgu", "w_s", "b_s")
MIXER_SHAPES = {
    "g_mix": (1, D), "w_in": (D, P_G), "w_in_g": (D, 2 * GMLP_W), "g_q": (1, Q_RANK),
    "w_uq": (Q_RANK, HEADS // 2 * Q_PAIR), "g_kv": (1, KV_RANK), "w_uk": (KV_RANK, HEADS * QK_NOPE),
    "w_uv": (KV_RANK, HEADS * V_DIM), "w_pool": (POOL_W, POOL_W), "pool_scale": (1, POOL_W),
    "g_sgu": (1, GMLP_W), "w_s": (GMLP_G * CHUNK, CHUNK), "b_s": (CHUNK, GMLP_W),
}
FFN_PARAMS = ("w_out", "g_ffn", "w_ff1", "w_ff2")
FFN_SHAPES = {"w_out": (D, D), "g_ffn": (1, D), "w_ff1": (D, D_FF), "w_ff2": (D_FF, D)}
SINGLE_BUFFERED = ("w_in", "w_out", "w_ff1", "w_ff2")


def _params(vmem_mib):
    return pltpu.CompilerParams(
        dimension_semantics=("arbitrary", "arbitrary"),
        vmem_limit_bytes=vmem_mib * MIB,
    )


def _rms(x, g):
    y = x * jax.lax.rsqrt(jnp.mean(x * x, axis=-1, keepdims=True) + EPS)
    return y * g


def _ada_rms(x, g, scale, shift):
    y = x * jax.lax.rsqrt(jnp.mean(x * x, axis=-1, keepdims=True) + EPS)
    return y * (g * (1.0 + scale)) + shift


def _dot(a, b):
    return jnp.dot(a, b, preferred_element_type=F32)


def _dot_nt(a, b):
    return jax.lax.dot_general(a, b, NT_DIMS, preferred_element_type=F32)


def _layer_spec(shape, layer, single=False):
    index_map = lambda bi, i: (layer,) + (0,) * len(shape)
    if single:
        return pl.BlockSpec((None,) + shape, index_map, pipeline_mode=pl.Buffered(1))
    return pl.BlockSpec((None,) + shape, index_map)


def _param_specs(names, shapes, layer):
    return [_layer_spec(shapes[n], layer, n in SINGLE_BUFFERED) for n in names]


def _mod_spec(layer, row0, per_batch):
    return pl.BlockSpec((None, None, 1, 6 * D), lambda bi, i: (layer, bi * per_batch + row0, 0, 0))


def _ada_kernel(c_ref, w_ref, b_ref, o_ref):
    c = c_ref[...]
    s = c / (1.0 + jnp.exp(-c))
    o_ref[0] = _dot(s.astype(BF16), w_ref[0].astype(BF16)) + b_ref[0]


def _ada(c_all, w_ada, b_ada):
    rows = c_all.shape[0]
    tn = 1536
    return pl.pallas_call(
        _ada_kernel,
        grid=(DEPTH, 6 * D // tn),
        in_specs=[
            pl.BlockSpec((rows, D), lambda l, j: (0, 0)),
            pl.BlockSpec((1, D, tn), lambda l, j: (l, 0, j)),
            pl.BlockSpec((1, 1, tn), lambda l, j: (l, 0, j)),
        ],
        out_specs=pl.BlockSpec((1, rows, tn), lambda l, j: (l, 0, j)),
        out_shape=jax.ShapeDtypeStruct((DEPTH, rows, 6 * D), F32),
        compiler_params=_params(32),
        name="ada",
    )(c_all, w_ada, b_ada.reshape(DEPTH, 1, 6 * D))


def _win_prep_kernel(w_ref, wa_ref, wg_ref):
    lane = jax.lax.broadcasted_iota(jnp.int32, (1, P_G - P_R), 1)
    wa_ref[:, 0:P_R] = w_ref[:, 0:P_R].astype(BF16)
    wa_ref[:, P_R:P_G] = jnp.where(lane < QK_ROPE, w_ref[:, P_R:P_G], 0.0).astype(BF16)
    wg_ref[...] = w_ref[:, P_R + QK_ROPE:].astype(BF16)


def _win_prep(w_in):
    cols = w_in.shape[2]
    return pl.pallas_call(
        _win_prep_kernel,
        grid=(DEPTH,),
        in_specs=[pl.BlockSpec((None, D, cols), lambda l: (l, 0, 0))],
        out_specs=[
            pl.BlockSpec((None, D, P_G), lambda l: (l, 0, 0)),
            pl.BlockSpec((None, D, 2 * GMLP_W), lambda l: (l, 0, 0)),
        ],
        out_shape=[
            jax.ShapeDtypeStruct((DEPTH, D, P_G), BF16),
            jax.ShapeDtypeStruct((DEPTH, D, 2 * GMLP_W), BF16),
        ],
        compiler_params=pltpu.CompilerParams(dimension_semantics=("arbitrary",), vmem_limit_bytes=32 * MIB),
        name="win_prep",
    )(w_in)


def _cache_kernel(ckv_ref, kr_ref, wk_ref, wv_ref, k_ref, v_ref):
    for s in range(ckv_ref.shape[0]):
        ckv = ckv_ref[s, 0].astype(BF16)
        kn = _dot(ckv, wk_ref[...])
        v = _dot(ckv, wv_ref[...])
        ones = jnp.ones((v.shape[0], V_PAD - V_DIM), BF16)
        kr = kr_ref[s, 0]
        kr_tiles = (kr.astype(BF16), pltpu.roll(kr, QK_ROPE, axis=1).astype(BF16))
        for h in range(HEADS):
            k_ref[s, 0, h] = jnp.concatenate(
                [kn[:, h * QK_NOPE:(h + 1) * QK_NOPE].astype(BF16), kr_tiles[h % 2]], axis=1)
            v_ref[s, 0, h] = jnp.concatenate([v[:, h * V_DIM:(h + 1) * V_DIM].astype(BF16), ones], axis=1)


def _cache_expand(cache_ckv, cache_kr_pad, w_uk, w_uv):
    b, depth, p, _ = cache_ckv.shape
    group = 4
    return pl.pallas_call(
        _cache_kernel,
        grid=(depth, b // group),
        in_specs=[
            pl.BlockSpec((group, 1, p, KV_RANK), lambda l, i: (i, l, 0, 0)),
            pl.BlockSpec((group, 1, p, 128), lambda l, i: (i, l, 0, 0)),
            pl.BlockSpec((None, KV_RANK, HEADS * QK_NOPE), lambda l, i: (l, 0, 0)),
            pl.BlockSpec((None, KV_RANK, HEADS * V_DIM), lambda l, i: (l, 0, 0)),
        ],
        out_specs=[
            pl.BlockSpec((group, 1, HEADS, p, QK_PAD), lambda l, i: (i, l, 0, 0, 0)),
            pl.BlockSpec((group, 1, HEADS, p, V_PAD), lambda l, i: (i, l, 0, 0, 0)),
        ],
        out_shape=[
            jax.ShapeDtypeStruct((b, depth, HEADS, p, QK_PAD), BF16),
            jax.ShapeDtypeStruct((b, depth, HEADS, p, V_PAD), BF16),
        ],
        compiler_params=_params(32),
        name="cache_expand",
    )(cache_ckv, cache_kr_pad, w_uk, w_uv)


def _rows_up(a, k):
    return pltpu.roll(a, a.shape[0] - k, axis=0)


def _rows_down(a, k):
    return pltpu.roll(a, k, axis=0)


def _rope(x, c, s1, s2):
    return x * c + pltpu.roll(x, 96, axis=1) * s1 + pltpu.roll(x, 32, axis=1) * s2


def _pool_mix(hp_e, hp, t0, seq, r):
    tm = hp.shape[0]
    main = slice(HALO, HALO + tm)
    lane_grp = jax.lax.broadcasted_iota(jnp.int32, (1, 2 * POOL_GD), 1) // POOL_GD
    xa = hp_e[:, 0:2 * POOL_GD]
    s2 = xa + _rows_down(xa, 1)
    s4 = _rows_up(s2, 1) + _rows_down(s2, 1)
    win_a = jnp.where(lane_grp == 0, s2[main], s4[main])
    xb = hp_e[:, 2 * POOL_GD:POOL_W]
    f2 = xb + _rows_up(xb, 1)
    f4 = f2 + _rows_up(f2, 2)
    f8 = f4 + _rows_up(f4, 4)
    s8 = _rows_down(f8, 4)
    s16 = f8[0:tm] + f8[main]
    win_b = jnp.where(lane_grp == 0, s8[main], s16)
    win = jnp.concatenate([win_a, win_b], axis=1)
    grp = jax.lax.broadcasted_iota(jnp.int32, (1, POOL_W), 1) // POOL_GD
    half = jnp.where(grp == 0, 1, jnp.where(grp == 1, 2, jnp.where(grp == 2, 4, 8)))
    t = t0 + jax.lax.broadcasted_iota(jnp.int32, (tm, 1), 0)
    cnt = jnp.clip(t + half, 0, seq) - jnp.clip(t - half, 0, seq)
    pooled = win / cnt.astype(F32) - hp
    return _dot(pooled.astype(BF16), r["w_pool"][...]) * r["pool_scale"][...]


def _gmlp(proj_g, r):
    tm = proj_g.shape[0]
    uv = jax.nn.gelu(proj_g)
    u = uv[:, 0:GMLP_W]
    vg = _rms(uv[:, GMLP_W:], r["g_sgu"][...]).astype(BF16)
    ggrp = jax.lax.broadcasted_iota(jnp.int32, (1, GMLP_W), 1) // GMLP_GD
    ys = []
    for cix in range(tm // CHUNK):
        m = _dot(r["w_s"][...], vg[cix * CHUNK:(cix + 1) * CHUNK])
        mixed = jnp.where(
            ggrp == 0, m[0:CHUNK],
            jnp.where(ggrp == 1, m[CHUNK:2 * CHUNK],
                      jnp.where(ggrp == 2, m[2 * CHUNK:3 * CHUNK], m[3 * CHUNK:4 * CHUNK])))
        ys.append(u[cix * CHUNK:(cix + 1) * CHUNK] * (mixed + r["b_s"][...]))
    return jnp.concatenate(ys, axis=0)


def _mla_operands(proj, r, rope):
    cq = _rms(proj[:, P_Q:P_KV], r["g_q"][...])
    q = _dot(cq.astype(BF16), r["w_uq"][...])
    ckv = _rms(proj[:, P_KV:P_R], r["g_kv"][...])
    ckv_b = ckv.astype(BF16)
    kn = _dot(ckv_b, r["w_uk"][...])
    v = _dot(ckv_b, r["w_uv"][...])
    ones = jnp.ones((v.shape[0], V_PAD - V_DIM), BF16)
    kr_raw = proj[:, P_R:P_G]
    kr = kr_raw if rope is None else _rope(kr_raw, *rope)
    kr_tiles = (kr.astype(BF16), pltpu.roll(kr, QK_ROPE, axis=1).astype(BF16))
    qs, ks, vs = [], [], []
    for pair in range(HEADS // 2):
        base = pair * Q_PAIR
        qr = q[:, base + 2 * QK_NOPE:base + Q_PAIR]
        if rope is not None:
            qr = _rope(qr, *rope)
        qr_b = qr.astype(BF16)
        for j in range(2):
            hd = 2 * pair + j
            qn = q[:, base + j * QK_NOPE:base + (j + 1) * QK_NOPE]
            qs.append(jnp.concatenate([qn.astype(BF16), qr_b], axis=1))
            ks.append(jnp.concatenate([kn[:, hd * QK_NOPE:(hd + 1) * QK_NOPE].astype(BF16), kr_tiles[j]], axis=1))
            vs.append(jnp.concatenate([v[:, hd * V_DIM:(hd + 1) * V_DIM].astype(BF16), ones], axis=1))
    return qs, ks, vs, ckv, kr_raw


def _softmax_pv(s, sc, v, vc):
    m = jnp.max(s, axis=1, keepdims=True)
    if sc is not None:
        m = jnp.maximum(m, jnp.max(sc, axis=1, keepdims=True))
    oa = _dot(jnp.exp2(s - m).astype(BF16), v)
    if sc is not None:
        oa = oa + _dot(jnp.exp2(sc - m).astype(BF16), vc)
    return oa[:, 0:V_DIM] / oa[:, V_DIM:V_DIM + 1]


def _ffn_block(x, ycat, mod_ref, r, g_final):
    g1 = mod_ref[:, 2 * D:3 * D]
    sh2 = mod_ref[:, 3 * D:4 * D]
    sc2 = mod_ref[:, 4 * D:5 * D]
    g2 = mod_ref[:, 5 * D:6 * D]
    x1 = x + g1 * _dot(ycat, r["w_out"][...])
    h = _ada_rms(x1, r["g_ffn"][...], sc2, sh2).astype(BF16)
    acc = None
    for j in range(D_FF // FF_CHUNK):
        a = _dot(h, r["w_ff1"][:, j * FF_CHUNK:(j + 1) * FF_CHUNK])
        a = jnp.square(jnp.maximum(a, 0.0)).astype(BF16)
        part = _dot(a, r["w_ff2"][j * FF_CHUNK:(j + 1) * FF_CHUNK, :])
        acc = part if acc is None else acc + part
    out = x1 + g2 * acc
    if g_final is not None:
        out = _rms(out, g_final)
    return out


def _ctx_kernel(*refs, names, final):
    r = dict(zip(names, refs))
    mod_ref = r["mod"]
    group, seq, _ = r["x"].shape
    edge = jnp.zeros((HALO, POOL_W), F32)
    xs, ycats = [], []
    for s in range(group):
        x = r["x"][s]
        h = _ada_rms(x, r["g_mix"][...], mod_ref[:, D:2 * D], mod_ref[:, 0:D]).astype(BF16)
        proj = _dot(h, r["w_in"][...])
        proj_g = _dot(h, r["w_in_g"][...])
        hp = proj[:, 0:POOL_W]
        y_pool = _pool_mix(jnp.concatenate([edge, hp, edge], axis=0), hp, 0, seq, r)
        y_g = _gmlp(proj_g, r)
        qs, ks, vs, ckv, kr_raw = _mla_operands(proj, r, None)
        r["sckv"][s] = ckv
        r["skr"][s] = kr_raw[:, 0:QK_ROPE]
        heads = [_softmax_pv(_dot_nt(qs[hd], ks[hd]), None, vs[hd], None) for hd in range(HEADS)]
        xs.append(x)
        ycats.append(jnp.concatenate([y_pool] + heads + [y_g], axis=1).astype(BF16))
    out = _ffn_block(jnp.concatenate(xs, axis=0), jnp.concatenate(ycats, axis=0), mod_ref, r,
                     r["g_final"][...] if final else None)
    r["o"][...] = out.reshape(group, seq, D)


def _ctx_layer(x, mod, layer, w, g_final):
    b, seq, _ = x.shape
    final = g_final is not None
    names = ["x", "mod"] + list(MIXER_PARAMS) + list(FFN_PARAMS)
    in_specs = ([pl.BlockSpec((CTX_GROUP, seq, D), lambda bi, i: (bi, 0, 0)), _mod_spec(layer, 0, 0)]
                + _param_specs(MIXER_PARAMS, MIXER_SHAPES, layer)
                + _param_specs(FFN_PARAMS, FFN_SHAPES, layer))
    args = [x, mod] + [w[n] for n in MIXER_PARAMS + FFN_PARAMS]
    if final:
        names.append("g_final")
        in_specs.append(pl.BlockSpec((1, D), lambda bi, i: (0, 0)))
        args.append(g_final)
    names += ["o", "sckv", "skr"]
    return pl.pallas_call(
        functools.partial(_ctx_kernel, names=tuple(names), final=final),
        grid=(b // CTX_GROUP, 1),
        in_specs=in_specs,
        out_specs=[
            pl.BlockSpec((CTX_GROUP, seq, D), lambda bi, i: (bi, 0, 0)),
            pl.BlockSpec((CTX_GROUP, seq, KV_RANK), lambda bi, i: (bi, 0, 0)),
            pl.BlockSpec((CTX_GROUP, seq, QK_ROPE), lambda bi, i: (bi, 0, 0)),
        ],
        out_shape=[
            jax.ShapeDtypeStruct((b, seq, D), F32),
            jax.ShapeDtypeStruct((b, seq, KV_RANK), F32),
            jax.ShapeDtypeStruct((b, seq, QK_ROPE), F32),
        ],
        compiler_params=_params(52),
        name="ctx_layer",
    )(*args)


def _pre_kernel(*refs, names, tm, seq):
    r = dict(zip(names, refs))
    mod_ref = r["mod"]
    i = pl.program_id(1)
    rows = tm + 2 * HALO
    xe = jnp.concatenate([r["x_prev"][0], r["x"][0], r["x_next"][0]], axis=0)
    h = _ada_rms(xe, r["g_mix"][...], mod_ref[:, D:2 * D], mod_ref[:, 0:D]).astype(BF16)
    proj_e = _dot(h, r["w_in"][...])
    proj = proj_e[HALO:HALO + tm]
    proj_g = _dot(h, r["w_in_g"][...])[HALO:HALO + tm]

    pos = i * tm - HALO + jax.lax.broadcasted_iota(jnp.int32, (rows, 1), 0)
    hp_e = jnp.where((pos >= 0) & (pos < seq), proj_e[:, 0:POOL_W], 0.0)
    y_pool = _pool_mix(hp_e, proj[:, 0:POOL_W], i * tm, seq, r)

    rope = (r["rope_c"][...], r["rope_s1"][...], r["rope_s2"][...])
    qs, ks, vs, _, _ = _mla_operands(proj, r, rope)
    for hd in range(HEADS):
        r["q"][0, hd] = qs[hd]
        r["k"][0, hd] = ks[hd]
        r["v"][0, hd] = vs[hd]

    y_g = _gmlp(proj_g, r)
    r["ypg"][0] = jnp.concatenate([y_pool, y_g], axis=1).astype(BF16)


def _pre(x, mod, layer, w, rope_tabs, *, tm):
    b, seq, _ = x.shape
    hb = tm // HALO
    nhb = seq // HALO
    names = (["x", "x_prev", "x_next", "mod"] + list(MIXER_PARAMS) + ["rope_c", "rope_s1", "rope_s2"]
             + ["q", "k", "v", "ypg"])
    in_specs = ([
        pl.BlockSpec((1, tm, D), lambda bi, i: (bi, i, 0)),
        pl.BlockSpec((1, HALO, D), lambda bi, i: (bi, jnp.maximum(i * hb - 1, 0), 0)),
        pl.BlockSpec((1, HALO, D), lambda bi, i: (bi, jnp.minimum((i + 1) * hb, nhb - 1), 0)),
        _mod_spec(layer, 1, 1),
    ] + _param_specs(MIXER_PARAMS, MIXER_SHAPES, layer)
      + [pl.BlockSpec((tm, 128), lambda bi, i: (i, 0))] * 3)
    args = [x, x, x, mod] + [w[n] for n in MIXER_PARAMS] + list(rope_tabs)
    return pl.pallas_call(
        functools.partial(_pre_kernel, names=tuple(names), tm=tm, seq=seq),
        grid=(b, seq // tm),
        in_specs=in_specs,
        out_specs=[
            pl.BlockSpec((1, HEADS, tm, QK_PAD), lambda bi, i: (bi, 0, i, 0)),
            pl.BlockSpec((1, HEADS, tm, QK_PAD), lambda bi, i: (bi, 0, i, 0)),
            pl.BlockSpec((1, HEADS, tm, V_PAD), lambda bi, i: (bi, 0, i, 0)),
            pl.BlockSpec((1, tm, 2 * GMLP_W), lambda bi, i: (bi, i, 0)),
        ],
        out_shape=[
            jax.ShapeDtypeStruct((b, HEADS, seq, QK_PAD), BF16),
            jax.ShapeDtypeStruct((b, HEADS, seq, QK_PAD), BF16),
            jax.ShapeDtypeStruct((b, HEADS, seq, V_PAD), BF16),
            jax.ShapeDtypeStruct((b, seq, 2 * GMLP_W), BF16),
        ],
        compiler_params=_params(56),
        name="pre_lat",
    )(*args)


def _attn_kernel(q_ref, k_ref, v_ref, kc_ref, vc_ref, o_ref, *, tq):
    units = [(h, slice(u * Q_UNIT, (u + 1) * Q_UNIT)) for h in range(HEADS) for u in range(tq // Q_UNIT)]

    def scores(unit):
        h, rows = unit
        q = q_ref[0, h, rows, :]
        return _dot_nt(q, k_ref[0, h]), _dot_nt(q, kc_ref[0, 0, h])

    ahead = [scores(u) for u in units[:LOOKAHEAD]]
    for idx, (h, rows) in enumerate(units):
        s, sc = ahead.pop(0)
        if idx + LOOKAHEAD < len(units):
            ahead.append(scores(units[idx + LOOKAHEAD]))
        o = _softmax_pv(s, sc, v_ref[0, h], vc_ref[0, 0, h])
        o_ref[0, rows, h * V_DIM:(h + 1) * V_DIM] = o.astype(BF16)


def _attn(q, k, v, cache, layer, *, tq):
    b, _, seq, _ = q.shape
    kc, vc = cache
    p = kc.shape[3]
    return pl.pallas_call(
        functools.partial(_attn_kernel, tq=tq),
        grid=(b, seq // tq),
        in_specs=[
            pl.BlockSpec((1, HEADS, tq, QK_PAD), lambda bi, i: (bi, 0, i, 0)),
            pl.BlockSpec((1, HEADS, seq, QK_PAD), lambda bi, i: (bi, 0, 0, 0)),
            pl.BlockSpec((1, HEADS, seq, V_PAD), lambda bi, i: (bi, 0, 0, 0)),
            pl.BlockSpec((1, 1, HEADS, p, QK_PAD), lambda bi, i: (bi, layer, 0, 0, 0)),
            pl.BlockSpec((1, 1, HEADS, p, V_PAD), lambda bi, i: (bi, layer, 0, 0, 0)),
        ],
        out_specs=pl.BlockSpec((1, tq, HEADS * V_DIM), lambda bi, i: (bi, i, 0)),
        out_shape=jax.ShapeDtypeStruct((b, seq, HEADS * V_DIM), BF16),
        compiler_params=_params(48),
        name="attn_lat",
    )(q, k, v, kc, vc)


def _post_kernel(*refs, names, final):
    r = dict(zip(names, refs))
    ypg = r["ypg"][0]
    ycat = jnp.concatenate([ypg[:, 0:POOL_W], r["ymla"][0], ypg[:, POOL_W:]], axis=1)
    r["o"][0] = _ffn_block(r["x"][0], ycat, r["mod"], r, r["g_final"][...] if final else None)


def _post(x, ypg, ymla, mod, layer, w, g_final, *, tm):
    b, seq, _ = x.shape
    final = g_final is not None
    names = ["x", "ypg", "ymla", "mod"] + list(FFN_PARAMS)
    in_specs = [
        pl.BlockSpec((1, tm, D), lambda bi, i: (bi, i, 0)),
        pl.BlockSpec((1, tm, 2 * GMLP_W), lambda bi, i: (bi, i, 0)),
        pl.BlockSpec((1, tm, HEADS * V_DIM), lambda bi, i: (bi, i, 0)),
        _mod_spec(layer, 1, 1),
    ] + _param_specs(FFN_PARAMS, FFN_SHAPES, layer)
    args = [x, ypg, ymla, mod] + [w[n] for n in FFN_PARAMS]
    if final:
        names.append("g_final")
        in_specs.append(pl.BlockSpec((1, D), lambda bi, i: (0, 0)))
        args.append(g_final)
    names.append("o")
    return pl.pallas_call(
        functools.partial(_post_kernel, names=tuple(names), final=final),
        grid=(b, seq // tm),
        in_specs=in_specs,
        out_specs=pl.BlockSpec((1, tm, D), lambda bi, i: (bi, i, 0)),
        out_shape=jax.ShapeDtypeStruct((b, seq, D), F32),
        compiler_params=_params(56),
        name="post_lat",
    )(*args)


def _rope_tables(seq):
    rows = seq // GRID_W
    row = jnp.repeat(jnp.arange(rows, dtype=F32), GRID_W)
    col = jnp.tile(jnp.arange(GRID_W, dtype=F32), rows)
    n_freq = QK_ROPE // 4
    inv = 1.0 / (ROPE_THETA ** (jnp.arange(n_freq, dtype=F32) / n_freq))
    ang = jnp.concatenate([row[:, None] * inv, col[:, None] * inv], axis=-1)
    cos, sin = jnp.cos(ang), jnp.sin(ang)
    z = jnp.zeros_like(cos)
    c = jnp.concatenate([cos, cos, cos, cos], axis=-1)
    s1 = jnp.concatenate([-sin, z, -sin, z], axis=-1)
    s2 = jnp.concatenate([z, sin, z, sin], axis=-1)
    return c, s1, s2


def kernel(x_prompt, x_sample, cache_ckv, cache_krope, c, c_ctx, w_ada, b_ada, g_mix, w_in, w_pool,
           pool_scale, g_q, w_uq, g_kv, w_ukv, g_sgu, w_s, b_s, w_out, g_ffn, w_ff1, w_ff2, g_final):
    dec_b = x_sample.shape[0]
    w_in_p, w_in_g = _win_prep(w_in)
    w_uq_h = w_uq.astype(BF16).reshape(DEPTH, Q_RANK, HEADS // 2, 2, QK_NOPE + QK_ROPE)
    w_uq_p = jnp.concatenate([w_uq_h[..., 0, :QK_NOPE], w_uq_h[..., 1, :QK_NOPE],
                              w_uq_h[..., 0, QK_NOPE:], w_uq_h[..., 1, QK_NOPE:]], axis=-1)
    w_uq_p = w_uq_p.reshape(DEPTH, Q_RANK, HEADS // 2 * Q_PAIR)
    w_ukv_h = w_ukv.astype(BF16).reshape(DEPTH, KV_RANK, HEADS, QK_NOPE + V_DIM)
    w_uk = w_ukv_h[..., :QK_NOPE].reshape(DEPTH, KV_RANK, HEADS * QK_NOPE)
    w_uv = w_ukv_h[..., QK_NOPE:].reshape(DEPTH, KV_RANK, HEADS * V_DIM)
    w_pool_bd = jnp.zeros((DEPTH, POOL_W, POOL_W), BF16)
    for g in range(len(POOL_WINDOWS)):
        sl = slice(g * POOL_GD, (g + 1) * POOL_GD)
        w_pool_bd = w_pool_bd.at[:, sl, sl].set(w_pool[:, g].astype(BF16))
    weights = {
        "g_mix": g_mix.reshape(DEPTH, 1, D),
        "w_in": w_in_p,
        "w_in_g": w_in_g,
        "g_q": (g_q * SM_SCALE).reshape(DEPTH, 1, Q_RANK),
        "w_uq": w_uq_p,
        "g_kv": g_kv.reshape(DEPTH, 1, KV_RANK),
        "w_uk": w_uk,
        "w_uv": w_uv,
        "w_pool": w_pool_bd,
        "pool_scale": pool_scale.reshape(DEPTH, 1, POOL_W),
        "g_sgu": g_sgu.reshape(DEPTH, 1, GMLP_W),
        "w_s": w_s.reshape(DEPTH, GMLP_G * CHUNK, CHUNK).astype(BF16),
        "b_s": jnp.repeat(jnp.swapaxes(b_s, 1, 2), GMLP_GD, axis=2),
        "w_out": w_out.astype(BF16),
        "g_ffn": g_ffn.reshape(DEPTH, 1, D),
        "w_ff1": w_ff1.astype(BF16),
        "w_ff2": w_ff2.astype(BF16),
    }
    g_fin = g_final.reshape(1, D)
    rope_tabs = _rope_tables(x_sample.shape[1])

    n_rows = 16
    c_all = jnp.zeros((n_rows, D), F32).at[0].set(c_ctx).at[1:1 + dec_b].set(c)
    mod = _ada(c_all, w_ada, b_ada).reshape(DEPTH, n_rows, 1, 6 * D)

    cache_kr_pad = jnp.pad(cache_krope, ((0, 0), (0, 0), (0, 0), (0, 128 - QK_ROPE)))
    cache = _cache_expand(cache_ckv, cache_kr_pad, w_uk, w_uv)

    xp, xs = x_prompt, x_sample
    ckv_list, kr_list = [], []
    for l in range(DEPTH):
        g_last = g_fin if l == DEPTH - 1 else None
        xp, sckv, skr = _ctx_layer(xp, mod, l, weights, g_last)
        ckv_list.append(sckv)
        kr_list.append(skr)
        q, k, v, ypg = _pre(xs, mod, l, weights, rope_tabs, tm=1024)
        ymla = _attn(q, k, v, cache, l, tq=1024)
        xs = _post(xs, ypg, ymla, mod, l, weights, g_last, tm=1024)
    return xp, xs, jnp.stack(ckv_list, axis=1), jnp.stack(kr_list, axis=1)
```

```python
import functools
import math

import jax
import jax.numpy as jnp
from jax.experimental import pallas as pl
from jax.experimental.pallas import tpu as pltpu

D = 1024
DEPTH = 4
GRID_W = 64
EPS = 1e-6
POOL_W = 256
POOL_WINDOWS = (2, 4, 8, 16)
POOL_GD = 64
HALO = 8
QK_NOPE = 128
QK_ROPE = 64
V_DIM = 128
V_PAD = 2 * V_DIM
HEADS = 4
Q_RANK = 384
KV_RANK = 256
ROPE_THETA = 10000.0
CHUNK = 128
GMLP_W = 256
GMLP_G = 4
GMLP_GD = 64
D_FF = 4096
FF_CHUNK = 1024
P_Q = 256
P_KV = P_Q + Q_RANK
P_R = P_KV + KV_RANK
P_G = P_R + 128
QK_PAD = 256
Q_PAIR = 2 * QK_NOPE + 2 * QK_ROPE
Q_UNIT = 512
CTX_GROUP = 2
LOOKAHEAD = 2
SM_SCALE = math.log2(math.e) / math.sqrt(QK_NOPE + QK_ROPE)
NT_DIMS = (((1,), (1,)), ((), ()))

F32 = jnp.float32
BF16 = jnp.bfloat16
MIB = 1024 * 1024

MIXER_PARAMS = ("g_mix", "w_in", "w_in_g", "g_q", "w_uq", "g_kv", "w_uk", "w_uv",
                "w_pool", "pool_scale", "g_sgu", "w_s", "b_s")
MIXER_SHAPES = {
    "g_mix": (1, D), "w_in": (D, P_G), "w_in_g": (D, 2 * GMLP_W), "g_q": (1, Q_RANK),
    "w_uq": (Q_RANK, HEADS // 2 * Q_PAIR), "g_kv": (1, KV_RANK), "w_uk": (KV_RANK, HEADS * QK_NOPE),
    "w_uv": (KV_RANK, HEADS * V_DIM), "w_pool": (POOL_W, POOL_W), "pool_scale": (1, POOL_W),
    "g_sgu": (1, GMLP_W), "w_s": (GMLP_G * CHUNK, CHUNK), "b_s": (CHUNK, GMLP_W),
}
FFN_PARAMS = ("w_out", "g_ffn", "w_ff1", "w_ff2")
FFN_SHAPES = {"w_out": (D, D), "g_ffn": (1, D), "w_ff1": (D, D_FF), "w_ff2": (D_FF, D)}
SINGLE_BUFFERED = ("w_in", "w_out", "w_ff1", "w_ff2")


def _params(vmem_mib):
    return pltpu.CompilerParams(
        dimension_semantics=("arbitrary", "arbitrary"),
        vmem_limit_bytes=vmem_mib * MIB,
    )


def _rms(x, g):
    y = x * jax.lax.rsqrt(jnp.mean(x * x, axis=-1, keepdims=True) + EPS)
    return y * g


def _ada_rms(x, g, scale, shift):
    y = x * jax.lax.rsqrt(jnp.mean(x * x, axis=-1, keepdims=True) + EPS)
    return y * (g * (1.0 + scale)) + shift


def _dot(a, b):
    return jnp.dot(a, b, preferred_element_type=F32)


def _dot_nt(a, b):
    return jax.lax.dot_general(a, b, NT_DIMS, preferred_element_type=F32)


def _slab_spec(shape, layer, single=False):
    if layer is None:
        block, index_map = shape, (lambda bi, i: (0,) * len(shape))
    else:
        block, index_map = (None,) + shape, (lambda bi, i: (layer,) + (0,) * len(shape))
    if single:
        return pl.BlockSpec(block, index_map, pipeline_mode=pl.Buffered(1))
    return pl.BlockSpec(block, index_map)


def _param_specs(names, shapes, layer):
    return [_slab_spec(shapes[n], layer, n in SINGLE_BUFFERED) for n in names]


def _mod_spec(row0, per_batch):
    return pl.BlockSpec((None, 1, 6 * D), lambda bi, i: (bi * per_batch + row0, 0, 0))


def _modulation(c, w, b):
    s = c / (1.0 + jnp.exp(-c))
    return _dot(s.astype(BF16), w.astype(BF16)) + b


def _ada_kernel(c_ref, w_ref, b_ref, o_ref):
    o_ref[...] = _modulation(c_ref[...], w_ref[...], b_ref[...])


def _ada_first(c_all, w_ada, b_ada):
    rows = c_all.shape[0]
    tn = 1536
    return pl.pallas_call(
        _ada_kernel,
        grid=(1, 6 * D // tn),
        in_specs=[
            pl.BlockSpec((rows, D), lambda l, j: (0, 0)),
            pl.BlockSpec((None, D, tn), lambda l, j: (0, 0, j)),
            pl.BlockSpec((None, 1, tn), lambda l, j: (0, 0, j)),
        ],
        out_specs=pl.BlockSpec((rows, tn), lambda l, j: (0, j)),
        out_shape=jax.ShapeDtypeStruct((rows, 6 * D), F32),
        compiler_params=_params(32),
        name="ada",
    )(c_all, w_ada, b_ada)


def _win_prep_kernel(w_ref, wa_ref, wg_ref):
    lane = jax.lax.broadcasted_iota(jnp.int32, (1, P_G - P_R), 1)
    wa_ref[:, 0:P_R] = w_ref[:, 0:P_R].astype(BF16)
    wa_ref[:, P_R:P_G] = jnp.where(lane < QK_ROPE, w_ref[:, P_R:P_G], 0.0).astype(BF16)
    wg_ref[...] = w_ref[:, P_R + QK_ROPE:].astype(BF16)


def _win_prep(w_in):
    cols = w_in.shape[2]
    return pl.pallas_call(
        _win_prep_kernel,
        grid=(DEPTH,),
        in_specs=[pl.BlockSpec((None, D, cols), lambda l: (l, 0, 0))],
        out_specs=[
            pl.BlockSpec((None, D, P_G), lambda l: (l, 0, 0)),
            pl.BlockSpec((None, D, 2 * GMLP_W), lambda l: (l, 0, 0)),
        ],
        out_shape=[
            jax.ShapeDtypeStruct((DEPTH, D, P_G), BF16),
            jax.ShapeDtypeStruct((DEPTH, D, 2 * GMLP_W), BF16),
        ],
        compiler_params=pltpu.CompilerParams(dimension_semantics=("arbitrary",), vmem_limit_bytes=32 * MIB),
        name="win_prep",
    )(w_in)


def _cache_kernel(ckv_ref, kr_ref, wk_ref, wv_ref, k_ref, v_ref):
    for s in range(ckv_ref.shape[0]):
        ckv = ckv_ref[s, 0].astype(BF16)
        kn = _dot(ckv, wk_ref[...])
        v = _dot(ckv, wv_ref[...])
        ones = jnp.ones((v.shape[0], V_PAD - V_DIM), BF16)
        kr = kr_ref[s, 0]
        kr_tiles = (kr.astype(BF16), pltpu.roll(kr, QK_ROPE, axis=1).astype(BF16))
        for h in range(HEADS):
            k_ref[s, 0, h] = jnp.concatenate(
                [kn[:, h * QK_NOPE:(h + 1) * QK_NOPE].astype(BF16), kr_tiles[h % 2]], axis=1)
            v_ref[s, 0, h] = jnp.concatenate([v[:, h * V_DIM:(h + 1) * V_DIM].astype(BF16), ones], axis=1)


def _cache_expand(cache_ckv, cache_kr_pad, w_uk, w_uv):
    b, depth, p, _ = cache_ckv.shape
    group = 4
    return pl.pallas_call(
        _cache_kernel,
        grid=(depth, b // group),
        in_specs=[
            pl.BlockSpec((group, 1, p, KV_RANK), lambda l, i: (i, l, 0, 0)),
            pl.BlockSpec((group, 1, p, 128), lambda l, i: (i, l, 0, 0)),
            pl.BlockSpec((None, KV_RANK, HEADS * QK_NOPE), lambda l, i: (l, 0, 0)),
            pl.BlockSpec((None, KV_RANK, HEADS * V_DIM), lambda l, i: (l, 0, 0)),
        ],
        out_specs=[
            pl.BlockSpec((group, 1, HEADS, p, QK_PAD), lambda l, i: (i, l, 0, 0, 0)),
            pl.BlockSpec((group, 1, HEADS, p, V_PAD), lambda l, i: (i, l, 0, 0, 0)),
        ],
        out_shape=[
            jax.ShapeDtypeStruct((b, depth, HEADS, p, QK_PAD), BF16),
            jax.ShapeDtypeStruct((b, depth, HEADS, p, V_PAD), BF16),
        ],
        compiler_params=_params(32),
        name="cache_expand",
    )(cache_ckv, cache_kr_pad, w_uk, w_uv)


def _rows_up(a, k):
    return pltpu.roll(a, a.shape[0] - k, axis=0)


def _rows_down(a, k):
    return pltpu.roll(a, k, axis=0)


def _rope(x, c, s1, s2):
    return x * c + pltpu.roll(x, 96, axis=1) * s1 + pltpu.roll(x, 32, axis=1) * s2


def _pool_mix(hp_e, hp, t0, seq, r):
    tm = hp.shape[0]
    main = slice(HALO, HALO + tm)
    lane_grp = jax.lax.broadcasted_iota(jnp.int32, (1, 2 * POOL_GD), 1) // POOL_GD
    xa = hp_e[:, 0:2 * POOL_GD]
    s2 = xa + _rows_down(xa, 1)
    s4 = _rows_up(s2, 1) + _rows_down(s2, 1)
    win_a = jnp.where(lane_grp == 0, s2[main], s4[main])
    xb = hp_e[:, 2 * POOL_GD:POOL_W]
    f2 = xb + _rows_up(xb, 1)
    f4 = f2 + _rows_up(f2, 2)
    f8 = f4 + _rows_up(f4, 4)
    s8 = _rows_down(f8, 4)
    s16 = f8[0:tm] + f8[main]
    win_b = jnp.where(lane_grp == 0, s8[main], s16)
    win = jnp.concatenate([win_a, win_b], axis=1)
    grp = jax.lax.broadcasted_iota(jnp.int32, (1, POOL_W), 1) // POOL_GD
    half = jnp.where(grp == 0, 1, jnp.where(grp == 1, 2, jnp.where(grp == 2, 4, 8)))
    t = t0 + jax.lax.broadcasted_iota(jnp.int32, (tm, 1), 0)
    cnt = jnp.clip(t + half, 0, seq) - jnp.clip(t - half, 0, seq)
    pooled = win / cnt.astype(F32) - hp
    return _dot(pooled.astype(BF16), r["w_pool"][...]) * r["pool_scale"][...]


def _gmlp(proj_g, r):
    tm = proj_g.shape[0]
    uv = jax.nn.gelu(proj_g)
    u = uv[:, 0:GMLP_W]
    vg = _rms(uv[:, GMLP_W:], r["g_sgu"][...]).astype(BF16)
    ggrp = jax.lax.broadcasted_iota(jnp.int32, (1, GMLP_W), 1) // GMLP_GD
    ys = []
    for cix in range(tm // CHUNK):
        m = _dot(r["w_s"][...], vg[cix * CHUNK:(cix + 1) * CHUNK])
        mixed = jnp.where(
            ggrp == 0, m[0:CHUNK],
            jnp.where(ggrp == 1, m[CHUNK:2 * CHUNK],
                      jnp.where(ggrp == 2, m[2 * CHUNK:3 * CHUNK], m[3 * CHUNK:4 * CHUNK])))
        ys.append(u[cix * CHUNK:(cix + 1) * CHUNK] * (mixed + r["b_s"][...]))
    return jnp.concatenate(ys, axis=0)


def _mla_operands(proj, r, rope):
    cq = _rms(proj[:, P_Q:P_KV], r["g_q"][...])
    q = _dot(cq.astype(BF16), r["w_uq"][...])
    ckv = _rms(proj[:, P_KV:P_R], r["g_kv"][...])
    ckv_b = ckv.astype(BF16)
    kn = _dot(ckv_b, r["w_uk"][...])
    v = _dot(ckv_b, r["w_uv"][...])
    ones = jnp.ones((v.shape[0], V_PAD - V_DIM), BF16)
    kr_raw = proj[:, P_R:P_G]
    kr = kr_raw if rope is None else _rope(kr_raw, *rope)
    kr_tiles = (kr.astype(BF16), pltpu.roll(kr, QK_ROPE, axis=1).astype(BF16))
    qs, ks, vs = [], [], []
    for pair in range(HEADS // 2):
        base = pair * Q_PAIR
        qr = q[:, base + 2 * QK_NOPE:base + Q_PAIR]
        if rope is not None:
            qr = _rope(qr, *rope)
        qr_b = qr.astype(BF16)
        for j in range(2):
            hd = 2 * pair + j
            qn = q[:, base + j * QK_NOPE:base + (j + 1) * QK_NOPE]
            qs.append(jnp.concatenate([qn.astype(BF16), qr_b], axis=1))
            ks.append(jnp.concatenate([kn[:, hd * QK_NOPE:(hd + 1) * QK_NOPE].astype(BF16), kr_tiles[j]], axis=1))
            vs.append(jnp.concatenate([v[:, hd * V_DIM:(hd + 1) * V_DIM].astype(BF16), ones], axis=1))
    return qs, ks, vs, ckv, kr_raw


def _softmax_pv(s, sc, v, vc):
    m = jnp.max(s, axis=1, keepdims=True)
    if sc is not None:
        m = jnp.maximum(m, jnp.max(sc, axis=1, keepdims=True))
    oa = _dot(jnp.exp2(s - m).astype(BF16), v)
    if sc is not None:
        oa = oa + _dot(jnp.exp2(sc - m).astype(BF16), vc)
    return oa[:, 0:V_DIM] / oa[:, V_DIM:V_DIM + 1]


def _ffn_block(x, ycat, mod_ref, r, g_final):
    g1 = mod_ref[:, 2 * D:3 * D]
    sh2 = mod_ref[:, 3 * D:4 * D]
    sc2 = mod_ref[:, 4 * D:5 * D]
    g2 = mod_ref[:, 5 * D:6 * D]
    x1 = x + g1 * _dot(ycat, r["w_out"][...])
    h = _ada_rms(x1, r["g_ffn"][...], sc2, sh2).astype(BF16)
    acc = None
    for j in range(D_FF // FF_CHUNK):
        a = _dot(h, r["w_ff1"][:, j * FF_CHUNK:(j + 1) * FF_CHUNK])
        a = jnp.square(jnp.maximum(a, 0.0)).astype(BF16)
        part = _dot(a, r["w_ff2"][j * FF_CHUNK:(j + 1) * FF_CHUNK, :])
        acc = part if acc is None else acc + part
    out = x1 + g2 * acc
    if g_final is not None:
        out = _rms(out, g_final)
    return out


def _ctx_kernel(*refs, names, final):
    r = dict(zip(names, refs))
    mod_ref = r["mod"]
    group, seq, _ = r["x"].shape
    edge = jnp.zeros((HALO, POOL_W), F32)
    xs, ycats = [], []
    for s in range(group):
        x = r["x"][s]
        h = _ada_rms(x, r["g_mix"][...], mod_ref[:, D:2 * D], mod_ref[:, 0:D]).astype(BF16)
        proj = _dot(h, r["w_in"][...])
        proj_g = _dot(h, r["w_in_g"][...])
        hp = proj[:, 0:POOL_W]
        y_pool = _pool_mix(jnp.concatenate([edge, hp, edge], axis=0), hp, 0, seq, r)
        y_g = _gmlp(proj_g, r)
        qs, ks, vs, ckv, kr_raw = _mla_operands(proj, r, None)
        r["sckv"][s] = ckv
        r["skr"][s] = kr_raw[:, 0:QK_ROPE]
        heads = [_softmax_pv(_dot_nt(qs[hd], ks[hd]), None, vs[hd], None) for hd in range(HEADS)]
        xs.append(x)
        ycats.append(jnp.concatenate([y_pool] + heads + [y_g], axis=1).astype(BF16))
    out = _ffn_block(jnp.concatenate(xs, axis=0), jnp.concatenate(ycats, axis=0), mod_ref, r,
                     r["g_final"][...] if final else None)
    r["o"][...] = out.reshape(group, seq, D)


def _ctx_layer(x, mod, layer, w, ffn, g_final):
    b, seq, _ = x.shape
    final = g_final is not None
    names = ["x", "mod"] + list(MIXER_PARAMS) + list(FFN_PARAMS)
    in_specs = ([pl.BlockSpec((CTX_GROUP, seq, D), lambda bi, i: (bi, 0, 0)), _mod_spec(0, 0)]
                + _param_specs(MIXER_PARAMS, MIXER_SHAPES, layer)
                + _param_specs(FFN_PARAMS, FFN_SHAPES, None))
    args = [x, mod] + [w[n] for n in MIXER_PARAMS] + [ffn[n] for n in FFN_PARAMS]
    if final:
        names.append("g_final")
        in_specs.append(pl.BlockSpec((1, D), lambda bi, i: (0, 0)))
        args.append(g_final)
    names += ["o", "sckv", "skr"]
    return pl.pallas_call(
        functools.partial(_ctx_kernel, names=tuple(names), final=final),
        grid=(b // CTX_GROUP, 1),
        in_specs=in_specs,
        out_specs=[
            pl.BlockSpec((CTX_GROUP, seq, D), lambda bi, i: (bi, 0, 0)),
            pl.BlockSpec((CTX_GROUP, seq, KV_RANK), lambda bi, i: (bi, 0, 0)),
            pl.BlockSpec((CTX_GROUP, seq, QK_ROPE), lambda bi, i: (bi, 0, 0)),
        ],
        out_shape=[
            jax.ShapeDtypeStruct((b, seq, D), F32),
            jax.ShapeDtypeStruct((b, seq, KV_RANK), F32),
            jax.ShapeDtypeStruct((b, seq, QK_ROPE), F32),
        ],
        compiler_params=_params(52),
        name="ctx_layer",
    )(*args)


def _pre_kernel(*refs, names, tm, seq):
    r = dict(zip(names, refs))
    mod_ref = r["mod"]
    i = pl.program_id(1)
    rows = tm + 2 * HALO
    xe = jnp.concatenate([r["x_prev"][0], r["x"][0], r["x_next"][0]], axis=0)
    h = _ada_rms(xe, r["g_mix"][...], mod_ref[:, D:2 * D], mod_ref[:, 0:D]).astype(BF16)
    proj_e = _dot(h, r["w_in"][...])
    proj = proj_e[HALO:HALO + tm]
    proj_g = _dot(h, r["w_in_g"][...])[HALO:HALO + tm]

    pos = i * tm - HALO + jax.lax.broadcasted_iota(jnp.int32, (rows, 1), 0)
    hp_e = jnp.where((pos >= 0) & (pos < seq), proj_e[:, 0:POOL_W], 0.0)
    y_pool = _pool_mix(hp_e, proj[:, 0:POOL_W], i * tm, seq, r)

    rope = (r["rope_c"][...], r["rope_s1"][...], r["rope_s2"][...])
    qs, ks, vs, _, _ = _mla_operands(proj, r, rope)
    for hd in range(HEADS):
        r["q"][0, hd] = qs[hd]
        r["k"][0, hd] = ks[hd]
        r["v"][0, hd] = vs[hd]

    y_g = _gmlp(proj_g, r)
    r["ypg"][0] = jnp.concatenate([y_pool, y_g], axis=1).astype(BF16)


def _pre(x, mod, layer, w, rope_tabs, *, tm):
    b, seq, _ = x.shape
    hb = tm // HALO
    nhb = seq // HALO
    names = (["x", "x_prev", "x_next", "mod"] + list(MIXER_PARAMS) + ["rope_c", "rope_s1", "rope_s2"]
             + ["q", "k", "v", "ypg"])
    in_specs = ([
        pl.BlockSpec((1, tm, D), lambda bi, i: (bi, i, 0)),
        pl.BlockSpec((1, HALO, D), lambda bi, i: (bi, jnp.maximum(i * hb - 1, 0), 0)),
        pl.BlockSpec((1, HALO, D), lambda bi, i: (bi, jnp.minimum((i + 1) * hb, nhb - 1), 0)),
        _mod_spec(1, 1),
    ] + _param_specs(MIXER_PARAMS, MIXER_SHAPES, layer)
      + [pl.BlockSpec((tm, 128), lambda bi, i: (i, 0))] * 3)
    args = [x, x, x, mod] + [w[n] for n in MIXER_PARAMS] + list(rope_tabs)
    return pl.pallas_call(
        functools.partial(_pre_kernel, names=tuple(names), tm=tm, seq=seq),
        grid=(b, seq // tm),
        in_specs=in_specs,
        out_specs=[
            pl.BlockSpec((1, HEADS, tm, QK_PAD), lambda bi, i: (bi, 0, i, 0)),
            pl.BlockSpec((1, HEADS, tm, QK_PAD), lambda bi, i: (bi, 0, i, 0)),
            pl.BlockSpec((1, HEADS, tm, V_PAD), lambda bi, i: (bi, 0, i, 0)),
            pl.BlockSpec((1, tm, 2 * GMLP_W), lambda bi, i: (bi, i, 0)),
        ],
        out_shape=[
            jax.ShapeDtypeStruct((b, HEADS, seq, QK_PAD), BF16),
            jax.ShapeDtypeStruct((b, HEADS, seq, QK_PAD), BF16),
            jax.ShapeDtypeStruct((b, HEADS, seq, V_PAD), BF16),
            jax.ShapeDtypeStruct((b, seq, 2 * GMLP_W), BF16),
        ],
        compiler_params=_params(56),
        name="pre_lat",
    )(*args)


CAST_PARAMS = ("w_out", "w_ff1", "w_ff2")


def _attn_kernel(*refs, names, tq, prep_next):
    r = dict(zip(names, refs))
    q_ref, k_ref, v_ref, kc_ref, vc_ref, o_ref = (r[n] for n in ("q", "k", "v", "kc", "vc", "o"))
    units = [(h, slice(u * Q_UNIT, (u + 1) * Q_UNIT)) for h in range(HEADS) for u in range(tq // Q_UNIT)]

    def scores(unit):
        h, rows = unit
        q = q_ref[0, h, rows, :]
        return _dot_nt(q, k_ref[0, h]), _dot_nt(q, kc_ref[0, 0, h])

    ahead = [scores(u) for u in units[:LOOKAHEAD]]
    for idx, (h, rows) in enumerate(units):
        s, sc = ahead.pop(0)
        if idx + LOOKAHEAD < len(units):
            ahead.append(scores(units[idx + LOOKAHEAD]))
        o = _softmax_pv(s, sc, v_ref[0, h], vc_ref[0, 0, h])
        o_ref[0, rows, h * V_DIM:(h + 1) * V_DIM] = o.astype(BF16)

    if prep_next:
        r["mod_next"][...] = _modulation(r["c_all"][...], r["w_ada"][...], r["b_ada"][...])
        for n in CAST_PARAMS:
            r[n + "_next"][...] = r[n + "_f32"][...].astype(BF16)


def _attn(q, k, v, cache, layer, nxt, *, tq):
    b, _, seq, _ = q.shape
    nt = seq // tq
    steps = b * nt
    kc, vc = cache
    p = kc.shape[3]
    names = ["q", "k", "v", "kc", "vc"]
    in_specs = [
        pl.BlockSpec((1, HEADS, tq, QK_PAD), lambda bi, i: (bi, 0, i, 0)),
        pl.BlockSpec((1, HEADS, seq, QK_PAD), lambda bi, i: (bi, 0, 0, 0)),
        pl.BlockSpec((1, HEADS, seq, V_PAD), lambda bi, i: (bi, 0, 0, 0)),
        pl.BlockSpec((1, 1, HEADS, p, QK_PAD), lambda bi, i: (bi, layer, 0, 0, 0)),
        pl.BlockSpec((1, 1, HEADS, p, V_PAD), lambda bi, i: (bi, layer, 0, 0, 0)),
    ]
    args = [q, k, v, kc, vc]
    out_names = ["o"]
    out_specs = [pl.BlockSpec((1, tq, HEADS * V_DIM), lambda bi, i: (bi, i, 0))]
    out_shape = [jax.ShapeDtypeStruct((b, seq, HEADS * V_DIM), BF16)]
    if nxt is not None:
        rows = nxt["c_all"].shape[0]
        tn = 6 * D // steps
        names += ["c_all", "w_ada", "b_ada"]
        in_specs += [
            pl.BlockSpec((rows, D), lambda bi, i: (0, 0)),
            pl.BlockSpec((None, D, tn), lambda bi, i: (layer + 1, 0, bi * nt + i)),
            pl.BlockSpec((None, 1, tn), lambda bi, i: (layer + 1, 0, bi * nt + i)),
        ]
        args += [nxt["c_all"], nxt["w_ada"], nxt["b_ada"]]
        out_names.append("mod_next")
        out_specs.append(pl.BlockSpec((rows, tn), lambda bi, i: (0, bi * nt + i)))
        out_shape.append(jax.ShapeDtypeStruct((rows, 6 * D), F32))
        for n in CAST_PARAMS:
            k_dim, n_dim = FFN_SHAPES[n]
            tr = k_dim // steps
            names.append(n + "_f32")
            in_specs.append(pl.BlockSpec((None, tr, n_dim), lambda bi, i: (layer + 1, bi * nt + i, 0)))
            args.append(nxt[n])
            out_names.append(n + "_next")
            out_specs.append(pl.BlockSpec((tr, n_dim), lambda bi, i: (bi * nt + i, 0)))
            out_shape.append(jax.ShapeDtypeStruct((k_dim, n_dim), BF16))
    return pl.pallas_call(
        functools.partial(_attn_kernel, names=tuple(names + out_names), tq=tq, prep_next=nxt is not None),
        grid=(b, nt),
        in_specs=in_specs,
        out_specs=out_specs,
        out_shape=out_shape,
        compiler_params=_params(52),
        name="attn_lat",
    )(*args)


def _post_kernel(*refs, names, final):
    r = dict(zip(names, refs))
    ypg = r["ypg"][0]
    ycat = jnp.concatenate([ypg[:, 0:POOL_W], r["ymla"][0], ypg[:, POOL_W:]], axis=1)
    r["o"][0] = _ffn_block(r["x"][0], ycat, r["mod"], r, r["g_final"][...] if final else None)


def _post(x, ypg, ymla, mod, ffn, g_final, *, tm):
    b, seq, _ = x.shape
    final = g_final is not None
    names = ["x", "ypg", "ymla", "mod"] + list(FFN_PARAMS)
    in_specs = [
        pl.BlockSpec((1, tm, D), lambda bi, i: (bi, i, 0)),
        pl.BlockSpec((1, tm, 2 * GMLP_W), lambda bi, i: (bi, i, 0)),
        pl.BlockSpec((1, tm, HEADS * V_DIM), lambda bi, i: (bi, i, 0)),
        _mod_spec(1, 1),
    ] + _param_specs(FFN_PARAMS, FFN_SHAPES, None)
    args = [x, ypg, ymla, mod] + [ffn[n] for n in FFN_PARAMS]
    if final:
        names.append("g_final")
        in_specs.append(pl.BlockSpec((1, D), lambda bi, i: (0, 0)))
        args.append(g_final)
    names.append("o")
    return pl.pallas_call(
        functools.partial(_post_kernel, names=tuple(names), final=final),
        grid=(b, seq // tm),
        in_specs=in_specs,
        out_specs=pl.BlockSpec((1, tm, D), lambda bi, i: (bi, i, 0)),
        out_shape=jax.ShapeDtypeStruct((b, seq, D), F32),
        compiler_params=_params(56),
        name="post_lat",
    )(*args)


def _rope_tables(seq):
    rows = seq // GRID_W
    row = jnp.repeat(jnp.arange(rows, dtype=F32), GRID_W)
    col = jnp.tile(jnp.arange(GRID_W, dtype=F32), rows)
    n_freq = QK_ROPE // 4
    inv = 1.0 / (ROPE_THETA ** (jnp.arange(n_freq, dtype=F32) / n_freq))
    ang = jnp.concatenate([row[:, None] * inv, col[:, None] * inv], axis=-1)
    cos, sin = jnp.cos(ang), jnp.sin(ang)
    z = jnp.zeros_like(cos)
    c = jnp.concatenate([cos, cos, cos, cos], axis=-1)
    s1 = jnp.concatenate([-sin, z, -sin, z], axis=-1)
    s2 = jnp.concatenate([z, sin, z, sin], axis=-1)
    return c, s1, s2


def kernel(x_prompt, x_sample, cache_ckv, cache_krope, c, c_ctx, w_ada, b_ada, g_mix, w_in, w_pool,
           pool_scale, g_q, w_uq, g_kv, w_ukv, g_sgu, w_s, b_s, w_out, g_ffn, w_ff1, w_ff2, g_final):
    dec_b = x_sample.shape[0]
    w_in_p, w_in_g = _win_prep(w_in)
    w_uq_h = w_uq.astype(BF16).reshape(DEPTH, Q_RANK, HEADS // 2, 2, QK_NOPE + QK_ROPE)
    w_uq_p = jnp.concatenate([w_uq_h[..., 0, :QK_NOPE], w_uq_h[..., 1, :QK_NOPE],
                              w_uq_h[..., 0, QK_NOPE:], w_uq_h[..., 1, QK_NOPE:]], axis=-1)
    w_uq_p = w_uq_p.reshape(DEPTH, Q_RANK, HEADS // 2 * Q_PAIR)
    w_ukv_h = w_ukv.astype(BF16).reshape(DEPTH, KV_RANK, HEADS, QK_NOPE + V_DIM)
    w_uk = w_ukv_h[..., :QK_NOPE].reshape(DEPTH, KV_RANK, HEADS * QK_NOPE)
    w_uv = w_ukv_h[..., QK_NOPE:].reshape(DEPTH, KV_RANK, HEADS * V_DIM)
    w_pool_bd = jnp.zeros((DEPTH, POOL_W, POOL_W), BF16)
    for g in range(len(POOL_WINDOWS)):
        sl = slice(g * POOL_GD, (g + 1) * POOL_GD)
        w_pool_bd = w_pool_bd.at[:, sl, sl].set(w_pool[:, g].astype(BF16))
    weights = {
        "g_mix": g_mix.reshape(DEPTH, 1, D),
        "w_in": w_in_p,
        "w_in_g": w_in_g,
        "g_q": (g_q * SM_SCALE).reshape(DEPTH, 1, Q_RANK),
        "w_uq": w_uq_p,
        "g_kv": g_kv.reshape(DEPTH, 1, KV_RANK),
        "w_uk": w_uk,
        "w_uv": w_uv,
        "w_pool": w_pool_bd,
        "pool_scale": pool_scale.reshape(DEPTH, 1, POOL_W),
        "g_sgu": g_sgu.reshape(DEPTH, 1, GMLP_W),
        "w_s": w_s.reshape(DEPTH, GMLP_G * CHUNK, CHUNK).astype(BF16),
        "b_s": jnp.repeat(jnp.swapaxes(b_s, 1, 2), GMLP_GD, axis=2),
    }
    g_fin = g_final.reshape(1, D)
    rope_tabs = _rope_tables(x_sample.shape[1])

    n_rows = 16
    c_all = jnp.zeros((n_rows, D), F32).at[0].set(c_ctx).at[1:1 + dec_b].set(c)
    b_ada3 = b_ada.reshape(DEPTH, 1, 6 * D)

    cache_kr_pad = jnp.pad(cache_krope, ((0, 0), (0, 0), (0, 0), (0, 128 - QK_ROPE)))
    cache = _cache_expand(cache_ckv, cache_kr_pad, w_uk, w_uv)

    mod = _ada_first(c_all, w_ada, b_ada3).reshape(n_rows, 1, 6 * D)
    ffn = {"w_out": w_out[0].astype(BF16), "w_ff1": w_ff1[0].astype(BF16), "w_ff2": w_ff2[0].astype(BF16)}
    f32_params = {"c_all": c_all, "w_ada": w_ada, "b_ada": b_ada3, "w_out": w_out, "w_ff1": w_ff1, "w_ff2": w_ff2}

    xp, xs = x_prompt, x_sample
    ckv_list, kr_list = [], []
    for l in range(DEPTH):
        last = l == DEPTH - 1
        g_last = g_fin if last else None
        ffn["g_ffn"] = g_ffn[l].reshape(1, D)
        xp, sckv, skr = _ctx_layer(xp, mod, l, weights, ffn, g_last)
        ckv_list.append(sckv)
        kr_list.append(skr)
        q, k, v, ypg = _pre(xs, mod, l, weights, rope_tabs, tm=1024)
        outs = _attn(q, k, v, cache, l, None if last else f32_params, tq=1024)
        xs = _post(xs, ypg, outs[0], mod, ffn, g_last, tm=1024)
        if not last:
            mod = outs[1].reshape(n_rows, 1, 6 * D)
            ffn = dict(zip(CAST_PARAMS, outs[2:]))
    return xp, xs, jnp.stack(ckv_list, axis=1), jnp.stack(kr_list, axis=1)
```

```python
import functools
import math

import jax
import jax.numpy as jnp
from jax.experimental import pallas as pl
from jax.experimental.pallas import tpu as pltpu

D = 1024
DEPTH = 4
GRID_W = 64
EPS = 1e-6
POOL_W = 256
POOL_WINDOWS = (2, 4, 8, 16)
POOL_GD = 64
HALO = 8
QK_NOPE = 128
QK_ROPE = 64
V_DIM = 128
V_PAD = 2 * V_DIM
HEADS = 4
Q_RANK = 384
KV_RANK = 256
ROPE_THETA = 10000.0
CHUNK = 128
GMLP_W = 256
GMLP_G = 4
GMLP_GD = 64
D_FF = 4096
FF_CHUNK = 1024
P_Q = 256
P_KV = P_Q + Q_RANK
P_R = P_KV + KV_RANK
P_G = P_R + 128
QK_PAD = 256
Q_PAIR = 2 * QK_NOPE + 2 * QK_ROPE
Q_UNIT = 512
CTX_GROUP = 2
LOOKAHEAD = 1
SM_SCALE = math.log2(math.e) / math.sqrt(QK_NOPE + QK_ROPE)
NT_DIMS = (((1,), (1,)), ((), ()))

F32 = jnp.float32
BF16 = jnp.bfloat16
MIB = 1024 * 1024

MIXER_PARAMS = ("g_mix", "w_in", "w_in_g", "g_q", "w_uq", "g_kv", "w_uk", "w_uv",
                "w_pool", "pool_scale", "g_sgu", "w_s", "b_s")
MIXER_SHAPES = {
    "g_mix": (1, D), "w_in": (D, P_G), "w_in_g": (D, 2 * GMLP_W), "g_q": (1, Q_RANK),
    "w_uq": (Q_RANK, HEADS // 2 * Q_PAIR), "g_kv": (1, KV_RANK), "w_uk": (KV_RANK, HEADS * QK_NOPE),
    "w_uv": (KV_RANK, HEADS * V_DIM), "w_pool": (POOL_W, POOL_W), "pool_scale": (1, POOL_W),
    "g_sgu": (1, GMLP_W), "w_s": (GMLP_G * CHUNK, CHUNK), "b_s": (CHUNK, GMLP_W),
}
FFN_PARAMS = ("w_out", "g_ffn", "w_ff1", "w_ff2")
FFN_SHAPES = {"w_out": (D, D), "g_ffn": (1, D), "w_ff1": (D, D_FF), "w_ff2": (D_FF, D)}
SINGLE_BUFFERED = ("w_in", "w_out", "w_ff1", "w_ff2")


def _params(vmem_mib):
    return pltpu.CompilerParams(
        dimension_semantics=("arbitrary", "arbitrary"),
        vmem_limit_bytes=vmem_mib * MIB,
    )


def _rms(x, g):
    y = x * jax.lax.rsqrt(jnp.mean(x * x, axis=-1, keepdims=True) + EPS)
    return y * g


def _ada_rms(x, g, scale, shift):
    y = x * jax.lax.rsqrt(jnp.mean(x * x, axis=-1, keepdims=True) + EPS)
    return y * (g * (1.0 + scale)) + shift


def _dot(a, b):
    return jnp.dot(a, b, preferred_element_type=F32)


def _dot_nt(a, b):
    return jax.lax.dot_general(a, b, NT_DIMS, preferred_element_type=F32)


def _slab_spec(shape, layer, single=False):
    if layer is None:
        block, index_map = shape, (lambda bi, i: (0,) * len(shape))
    else:
        block, index_map = (None,) + shape, (lambda bi, i: (layer,) + (0,) * len(shape))
    if single:
        return pl.BlockSpec(block, index_map, pipeline_mode=pl.Buffered(1))
    return pl.BlockSpec(block, index_map)


def _param_specs(names, shapes, layer):
    return [_slab_spec(shapes[n], layer, n in SINGLE_BUFFERED) for n in names]


def _mod_spec(row0, per_batch):
    return pl.BlockSpec((None, 1, 6 * D), lambda bi, i: (bi * per_batch + row0, 0, 0))


def _modulation(c, w, b):
    s = c / (1.0 + jnp.exp(-c))
    return _dot(s.astype(BF16), w.astype(BF16)) + b


def _ada_kernel(c_ref, w_ref, b_ref, o_ref):
    o_ref[...] = _modulation(c_ref[...], w_ref[...], b_ref[...])


def _ada_first(c_all, w_ada, b_ada):
    rows = c_all.shape[0]
    tn = 1536
    return pl.pallas_call(
        _ada_kernel,
        grid=(1, 6 * D // tn),
        in_specs=[
            pl.BlockSpec((rows, D), lambda l, j: (0, 0)),
            pl.BlockSpec((None, D, tn), lambda l, j: (0, 0, j)),
            pl.BlockSpec((None, 1, tn), lambda l, j: (0, 0, j)),
        ],
        out_specs=pl.BlockSpec((rows, tn), lambda l, j: (0, j)),
        out_shape=jax.ShapeDtypeStruct((rows, 6 * D), F32),
        compiler_params=_params(32),
        name="ada",
    )(c_all, w_ada, b_ada)


def _win_prep_kernel(w_ref, wa_ref, wg_ref):
    lane = jax.lax.broadcasted_iota(jnp.int32, (1, P_G - P_R), 1)
    wa_ref[:, 0:P_R] = w_ref[:, 0:P_R].astype(BF16)
    wa_ref[:, P_R:P_G] = jnp.where(lane < QK_ROPE, w_ref[:, P_R:P_G], 0.0).astype(BF16)
    wg_ref[...] = w_ref[:, P_R + QK_ROPE:].astype(BF16)


def _win_prep(w_in):
    cols = w_in.shape[2]
    return pl.pallas_call(
        _win_prep_kernel,
        grid=(DEPTH,),
        in_specs=[pl.BlockSpec((None, D, cols), lambda l: (l, 0, 0))],
        out_specs=[
            pl.BlockSpec((None, D, P_G), lambda l: (l, 0, 0)),
            pl.BlockSpec((None, D, 2 * GMLP_W), lambda l: (l, 0, 0)),
        ],
        out_shape=[
            jax.ShapeDtypeStruct((DEPTH, D, P_G), BF16),
            jax.ShapeDtypeStruct((DEPTH, D, 2 * GMLP_W), BF16),
        ],
        compiler_params=pltpu.CompilerParams(dimension_semantics=("arbitrary",), vmem_limit_bytes=32 * MIB),
        name="win_prep",
    )(w_in)


def _cache_kernel(ckv_ref, kr_ref, wk_ref, wv_ref, k_ref, v_ref):
    for s in range(ckv_ref.shape[0]):
        ckv = ckv_ref[s, 0].astype(BF16)
        kn = _dot(ckv, wk_ref[...])
        v = _dot(ckv, wv_ref[...])
        ones = jnp.ones((v.shape[0], V_PAD - V_DIM), BF16)
        kr = kr_ref[s, 0]
        kr_tiles = (kr.astype(BF16), pltpu.roll(kr, QK_ROPE, axis=1).astype(BF16))
        for h in range(HEADS):
            k_ref[s, 0, h] = jnp.concatenate(
                [kn[:, h * QK_NOPE:(h + 1) * QK_NOPE].astype(BF16), kr_tiles[h % 2]], axis=1)
            v_ref[s, 0, h] = jnp.concatenate([v[:, h * V_DIM:(h + 1) * V_DIM].astype(BF16), ones], axis=1)


def _cache_expand(cache_ckv, cache_kr_pad, w_uk, w_uv):
    b, depth, p, _ = cache_ckv.shape
    group = 4
    return pl.pallas_call(
        _cache_kernel,
        grid=(depth, b // group),
        in_specs=[
            pl.BlockSpec((group, 1, p, KV_RANK), lambda l, i: (i, l, 0, 0)),
            pl.BlockSpec((group, 1, p, 128), lambda l, i: (i, l, 0, 0)),
            pl.BlockSpec((None, KV_RANK, HEADS * QK_NOPE), lambda l, i: (l, 0, 0)),
            pl.BlockSpec((None, KV_RANK, HEADS * V_DIM), lambda l, i: (l, 0, 0)),
        ],
        out_specs=[
            pl.BlockSpec((group, 1, HEADS, p, QK_PAD), lambda l, i: (i, l, 0, 0, 0)),
            pl.BlockSpec((group, 1, HEADS, p, V_PAD), lambda l, i: (i, l, 0, 0, 0)),
        ],
        out_shape=[
            jax.ShapeDtypeStruct((b, depth, HEADS, p, QK_PAD), BF16),
            jax.ShapeDtypeStruct((b, depth, HEADS, p, V_PAD), BF16),
        ],
        compiler_params=_params(32),
        name="cache_expand",
    )(cache_ckv, cache_kr_pad, w_uk, w_uv)


def _rows_up(a, k):
    return pltpu.roll(a, a.shape[0] - k, axis=0)


def _rows_down(a, k):
    return pltpu.roll(a, k, axis=0)


def _rope(x, c, s1, s2):
    return x * c + pltpu.roll(x, 96, axis=1) * s1 + pltpu.roll(x, 32, axis=1) * s2


def _pool_mix(hp_e, hp, t0, seq, r):
    tm = hp.shape[0]
    main = slice(HALO, HALO + tm)
    lane_grp = jax.lax.broadcasted_iota(jnp.int32, (1, 2 * POOL_GD), 1) // POOL_GD
    xa = hp_e[:, 0:2 * POOL_GD]
    s2 = xa + _rows_down(xa, 1)
    s4 = _rows_up(s2, 1) + _rows_down(s2, 1)
    win_a = jnp.where(lane_grp == 0, s2[main], s4[main])
    xb = hp_e[:, 2 * POOL_GD:POOL_W]
    f2 = xb + _rows_up(xb, 1)
    f4 = f2 + _rows_up(f2, 2)
    f8 = f4 + _rows_up(f4, 4)
    s8 = _rows_down(f8, 4)
    s16 = f8[0:tm] + f8[main]
    win_b = jnp.where(lane_grp == 0, s8[main], s16)
    win = jnp.concatenate([win_a, win_b], axis=1)
    grp = jax.lax.broadcasted_iota(jnp.int32, (1, POOL_W), 1) // POOL_GD
    half = jnp.where(grp == 0, 1, jnp.where(grp == 1, 2, jnp.where(grp == 2, 4, 8)))
    t = t0 + jax.lax.broadcasted_iota(jnp.int32, (tm, 1), 0)
    cnt = jnp.clip(t + half, 0, seq) - jnp.clip(t - half, 0, seq)
    pooled = win / cnt.astype(F32) - hp
    return _dot(pooled.astype(BF16), r["w_pool"][...]) * r["pool_scale"][...]


def _gmlp(proj_g, r):
    tm = proj_g.shape[0]
    uv = jax.nn.gelu(proj_g)
    u = uv[:, 0:GMLP_W]
    vg = _rms(uv[:, GMLP_W:], r["g_sgu"][...]).astype(BF16)
    ggrp = jax.lax.broadcasted_iota(jnp.int32, (1, GMLP_W), 1) // GMLP_GD
    ys = []
    for cix in range(tm // CHUNK):
        m = _dot(r["w_s"][...], vg[cix * CHUNK:(cix + 1) * CHUNK])
        mixed = jnp.where(
            ggrp == 0, m[0:CHUNK],
            jnp.where(ggrp == 1, m[CHUNK:2 * CHUNK],
                      jnp.where(ggrp == 2, m[2 * CHUNK:3 * CHUNK], m[3 * CHUNK:4 * CHUNK])))
        ys.append(u[cix * CHUNK:(cix + 1) * CHUNK] * (mixed + r["b_s"][...]))
    return jnp.concatenate(ys, axis=0)


def _mla_operands(proj, r, rope):
    cq = _rms(proj[:, P_Q:P_KV], r["g_q"][...])
    q = _dot(cq.astype(BF16), r["w_uq"][...])
    ckv = _rms(proj[:, P_KV:P_R], r["g_kv"][...])
    ckv_b = ckv.astype(BF16)
    kn = _dot(ckv_b, r["w_uk"][...])
    v = _dot(ckv_b, r["w_uv"][...])
    ones = jnp.ones((v.shape[0], V_PAD - V_DIM), BF16)
    kr_raw = proj[:, P_R:P_G]
    kr = kr_raw if rope is None else _rope(kr_raw, *rope)
    kr_tiles = (kr.astype(BF16), pltpu.roll(kr, QK_ROPE, axis=1).astype(BF16))
    qs, ks, vs = [], [], []
    for pair in range(HEADS // 2):
        base = pair * Q_PAIR
        qr = q[:, base + 2 * QK_NOPE:base + Q_PAIR]
        if rope is not None:
            qr = _rope(qr, *rope)
        qr_b = qr.astype(BF16)
        for j in range(2):
            hd = 2 * pair + j
            qn = q[:, base + j * QK_NOPE:base + (j + 1) * QK_NOPE]
            qs.append(jnp.concatenate([qn.astype(BF16), qr_b], axis=1))
            ks.append(jnp.concatenate([kn[:, hd * QK_NOPE:(hd + 1) * QK_NOPE].astype(BF16), kr_tiles[j]], axis=1))
            vs.append(jnp.concatenate([v[:, hd * V_DIM:(hd + 1) * V_DIM].astype(BF16), ones], axis=1))
    return qs, ks, vs, ckv, kr_raw


def _softmax_pv(s, sc, v, vc):
    m = jnp.max(s, axis=1, keepdims=True)
    if sc is not None:
        m = jnp.maximum(m, jnp.max(sc, axis=1, keepdims=True))
    oa = _dot(jnp.exp2(s - m).astype(BF16), v)
    if sc is not None:
        oa = oa + _dot(jnp.exp2(sc - m).astype(BF16), vc)
    return oa[:, 0:V_DIM] / oa[:, V_DIM:V_DIM + 1]


def _ffn_block(x, ycat, mod_ref, r, g_final):
    g1 = mod_ref[:, 2 * D:3 * D]
    sh2 = mod_ref[:, 3 * D:4 * D]
    sc2 = mod_ref[:, 4 * D:5 * D]
    g2 = mod_ref[:, 5 * D:6 * D]
    x1 = x + g1 * _dot(ycat, r["w_out"][...])
    h = _ada_rms(x1, r["g_ffn"][...], sc2, sh2).astype(BF16)
    acc = None
    for j in range(D_FF // FF_CHUNK):
        a = _dot(h, r["w_ff1"][:, j * FF_CHUNK:(j + 1) * FF_CHUNK])
        a = jnp.square(jnp.maximum(a, 0.0)).astype(BF16)
        part = _dot(a, r["w_ff2"][j * FF_CHUNK:(j + 1) * FF_CHUNK, :])
        acc = part if acc is None else acc + part
    out = x1 + g2 * acc
    if g_final is not None:
        out = _rms(out, g_final)
    return out


def _ctx_kernel(*refs, names, final):
    r = dict(zip(names, refs))
    mod_ref = r["mod"]
    group, seq, _ = r["x"].shape
    x = r["x"][...].reshape(group * seq, D)
    h = _ada_rms(x, r["g_mix"][...], mod_ref[:, D:2 * D], mod_ref[:, 0:D]).astype(BF16)
    proj = _dot(h, r["w_in"][...])
    proj_g = _dot(h, r["w_in_g"][...])
    y_g = _gmlp(proj_g, r)
    qs, ks, vs, ckv, kr_raw = _mla_operands(proj, r, None)
    r["sckv"][...] = ckv.reshape(group, seq, KV_RANK)
    r["skr"][...] = kr_raw[:, 0:QK_ROPE].reshape(group, seq, QK_ROPE)
    edge = jnp.zeros((HALO, POOL_W), F32)
    ycats = []
    for s in range(group):
        own = slice(s * seq, (s + 1) * seq)
        hp = proj[own, 0:POOL_W]
        y_pool = _pool_mix(jnp.concatenate([edge, hp, edge], axis=0), hp, 0, seq, r)
        heads = [_softmax_pv(_dot_nt(qs[hd][own], ks[hd][own]), None, vs[hd][own], None)
                 for hd in range(HEADS)]
        ycats.append(jnp.concatenate([y_pool] + heads + [y_g[own]], axis=1).astype(BF16))
    out = _ffn_block(x, jnp.concatenate(ycats, axis=0), mod_ref, r, r["g_final"][...] if final else None)
    r["o"][...] = out.reshape(group, seq, D)


def _ctx_layer(x, mod, layer, w, ffn, g_final):
    b, seq, _ = x.shape
    final = g_final is not None
    names = ["x", "mod"] + list(MIXER_PARAMS) + list(FFN_PARAMS)
    in_specs = ([pl.BlockSpec((CTX_GROUP, seq, D), lambda bi, i: (bi, 0, 0)), _mod_spec(0, 0)]
                + _param_specs(MIXER_PARAMS, MIXER_SHAPES, layer)
                + _param_specs(FFN_PARAMS, FFN_SHAPES, None))
    args = [x, mod] + [w[n] for n in MIXER_PARAMS] + [ffn[n] for n in FFN_PARAMS]
    if final:
        names.append("g_final")
        in_specs.append(pl.BlockSpec((1, D), lambda bi, i: (0, 0)))
        args.append(g_final)
    names += ["o", "sckv", "skr"]
    return pl.pallas_call(
        functools.partial(_ctx_kernel, names=tuple(names), final=final),
        grid=(b // CTX_GROUP, 1),
        in_specs=in_specs,
        out_specs=[
            pl.BlockSpec((CTX_GROUP, seq, D), lambda bi, i: (bi, 0, 0)),
            pl.BlockSpec((CTX_GROUP, seq, KV_RANK), lambda bi, i: (bi, 0, 0)),
            pl.BlockSpec((CTX_GROUP, seq, QK_ROPE), lambda bi, i: (bi, 0, 0)),
        ],
        out_shape=[
            jax.ShapeDtypeStruct((b, seq, D), F32),
            jax.ShapeDtypeStruct((b, seq, KV_RANK), F32),
            jax.ShapeDtypeStruct((b, seq, QK_ROPE), F32),
        ],
        compiler_params=_params(52),
        name="ctx_layer",
    )(*args)


def _pre_kernel(*refs, names, tm, seq):
    r = dict(zip(names, refs))
    mod_ref = r["mod"]
    i = pl.program_id(1)
    rows = tm + 2 * HALO
    xe = jnp.concatenate([r["x_prev"][0], r["x"][0], r["x_next"][0]], axis=0)
    h = _ada_rms(xe, r["g_mix"][...], mod_ref[:, D:2 * D], mod_ref[:, 0:D]).astype(BF16)
    proj_e = _dot(h, r["w_in"][...])
    proj = proj_e[HALO:HALO + tm]
    proj_g = _dot(h, r["w_in_g"][...])[HALO:HALO + tm]

    pos = i * tm - HALO + jax.lax.broadcasted_iota(jnp.int32, (rows, 1), 0)
    hp_e = jnp.where((pos >= 0) & (pos < seq), proj_e[:, 0:POOL_W], 0.0)
    y_pool = _pool_mix(hp_e, proj[:, 0:POOL_W], i * tm, seq, r)

    rope = (r["rope_c"][...], r["rope_s1"][...], r["rope_s2"][...])
    qs, ks, vs, _, _ = _mla_operands(proj, r, rope)
    for hd in range(HEADS):
        r["q"][0, hd] = qs[hd]
        r["k"][0, hd] = ks[hd]
        r["v"][0, hd] = vs[hd]

    y_g = _gmlp(proj_g, r)
    r["ypg"][0] = jnp.concatenate([y_pool, y_g], axis=1).astype(BF16)


def _pre(x, mod, layer, w, rope_tabs, *, tm):
    b, seq, _ = x.shape
    hb = tm // HALO
    nhb = seq // HALO
    names = (["x", "x_prev", "x_next", "mod"] + list(MIXER_PARAMS) + ["rope_c", "rope_s1", "rope_s2"]
             + ["q", "k", "v", "ypg"])
    in_specs = ([
        pl.BlockSpec((1, tm, D), lambda bi, i: (bi, i, 0)),
        pl.BlockSpec((1, HALO, D), lambda bi, i: (bi, jnp.maximum(i * hb - 1, 0), 0)),
        pl.BlockSpec((1, HALO, D), lambda bi, i: (bi, jnp.minimum((i + 1) * hb, nhb - 1), 0)),
        _mod_spec(1, 1),
    ] + _param_specs(MIXER_PARAMS, MIXER_SHAPES, layer)
      + [pl.BlockSpec((tm, 128), lambda bi, i: (i, 0))] * 3)
    args = [x, x, x, mod] + [w[n] for n in MIXER_PARAMS] + list(rope_tabs)
    return pl.pallas_call(
        functools.partial(_pre_kernel, names=tuple(names), tm=tm, seq=seq),
        grid=(b, seq // tm),
        in_specs=in_specs,
        out_specs=[
            pl.BlockSpec((1, HEADS, tm, QK_PAD), lambda bi, i: (bi, 0, i, 0)),
            pl.BlockSpec((1, HEADS, tm, QK_PAD), lambda bi, i: (bi, 0, i, 0)),
            pl.BlockSpec((1, HEADS, tm, V_PAD), lambda bi, i: (bi, 0, i, 0)),
            pl.BlockSpec((1, tm, 2 * GMLP_W), lambda bi, i: (bi, i, 0)),
        ],
        out_shape=[
            jax.ShapeDtypeStruct((b, HEADS, seq, QK_PAD), BF16),
            jax.ShapeDtypeStruct((b, HEADS, seq, QK_PAD), BF16),
            jax.ShapeDtypeStruct((b, HEADS, seq, V_PAD), BF16),
            jax.ShapeDtypeStruct((b, seq, 2 * GMLP_W), BF16),
        ],
        compiler_params=_params(56),
        name="pre_lat",
    )(*args)


CAST_PARAMS = ("w_out", "w_ff1", "w_ff2")


def _attn_kernel(*refs, names, tq, prep_next):
    r = dict(zip(names, refs))
    q_ref, k_ref, v_ref, kc_ref, vc_ref, o_ref = (r[n] for n in ("q", "k", "v", "kc", "vc", "o"))
    units = [(h, slice(u * Q_UNIT, (u + 1) * Q_UNIT)) for h in range(HEADS) for u in range(tq // Q_UNIT)]

    def scores(unit):
        h, rows = unit
        q = q_ref[0, h, rows, :]
        return _dot_nt(q, k_ref[0, h]), _dot_nt(q, kc_ref[0, 0, h])

    ahead = [scores(u) for u in units[:LOOKAHEAD]]
    for idx, (h, rows) in enumerate(units):
        s, sc = ahead.pop(0)
        if idx + LOOKAHEAD < len(units):
            ahead.append(scores(units[idx + LOOKAHEAD]))
        o = _softmax_pv(s, sc, v_ref[0, h], vc_ref[0, 0, h])
        o_ref[0, rows, h * V_DIM:(h + 1) * V_DIM] = o.astype(BF16)

    if prep_next:
        r["mod_next"][...] = _modulation(r["c_all"][...], r["w_ada"][...], r["b_ada"][...])
        for n in CAST_PARAMS:
            r[n + "_next"][...] = r[n + "_f32"][...].astype(BF16)


def _attn(q, k, v, cache, layer, nxt, *, tq):
    b, _, seq, _ = q.shape
    nt = seq // tq
    steps = b * nt
    kc, vc = cache
    p = kc.shape[3]
    names = ["q", "k", "v", "kc", "vc"]
    in_specs = [
        pl.BlockSpec((1, HEADS, tq, QK_PAD), lambda bi, i: (bi, 0, i, 0)),
        pl.BlockSpec((1, HEADS, seq, QK_PAD), lambda bi, i: (bi, 0, 0, 0)),
        pl.BlockSpec((1, HEADS, seq, V_PAD), lambda bi, i: (bi, 0, 0, 0)),
        pl.BlockSpec((1, 1, HEADS, p, QK_PAD), lambda bi, i: (bi, layer, 0, 0, 0)),
        pl.BlockSpec((1, 1, HEADS, p, V_PAD), lambda bi, i: (bi, layer, 0, 0, 0)),
    ]
    args = [q, k, v, kc, vc]
    out_names = ["o"]
    out_specs = [pl.BlockSpec((1, tq, HEADS * V_DIM), lambda bi, i: (bi, i, 0))]
    out_shape = [jax.ShapeDtypeStruct((b, seq, HEADS * V_DIM), BF16)]
    if nxt is not None:
        rows = nxt["c_all"].shape[0]
        tn = 6 * D // steps
        names += ["c_all", "w_ada", "b_ada"]
        in_specs += [
            pl.BlockSpec((rows, D), lambda bi, i: (0, 0)),
            pl.BlockSpec((None, D, tn), lambda bi, i: (layer + 1, 0, bi * nt + i)),
            pl.BlockSpec((None, 1, tn), lambda bi, i: (layer + 1, 0, bi * nt + i)),
        ]
        args += [nxt["c_all"], nxt["w_ada"], nxt["b_ada"]]
        out_names.append("mod_next")
        out_specs.append(pl.BlockSpec((rows, tn), lambda bi, i: (0, bi * nt + i)))
        out_shape.append(jax.ShapeDtypeStruct((rows, 6 * D), F32))
        for n in CAST_PARAMS:
            k_dim, n_dim = FFN_SHAPES[n]
            tr = k_dim // steps
            names.append(n + "_f32")
            in_specs.append(pl.BlockSpec((None, tr, n_dim), lambda bi, i: (layer + 1, bi * nt + i, 0)))
            args.append(nxt[n])
            out_names.append(n + "_next")
            out_specs.append(pl.BlockSpec((tr, n_dim), lambda bi, i: (bi * nt + i, 0)))
            out_shape.append(jax.ShapeDtypeStruct((k_dim, n_dim), BF16))
    return pl.pallas_call(
        functools.partial(_attn_kernel, names=tuple(names + out_names), tq=tq, prep_next=nxt is not None),
        grid=(b, nt),
        in_specs=in_specs,
        out_specs=out_specs,
        out_shape=out_shape,
        compiler_params=_params(52),
        name="attn_lat",
    )(*args)


def _post_kernel(*refs, names, final):
    r = dict(zip(names, refs))
    ypg = r["ypg"][0]
    ycat = jnp.concatenate([ypg[:, 0:POOL_W], r["ymla"][0], ypg[:, POOL_W:]], axis=1)
    r["o"][0] = _ffn_block(r["x"][0], ycat, r["mod"], r, r["g_final"][...] if final else None)


def _post(x, ypg, ymla, mod, ffn, g_final, *, tm):
    b, seq, _ = x.shape
    final = g_final is not None
    names = ["x", "ypg", "ymla", "mod"] + list(FFN_PARAMS)
    in_specs = [
        pl.BlockSpec((1, tm, D), lambda bi, i: (bi, i, 0)),
        pl.BlockSpec((1, tm, 2 * GMLP_W), lambda bi, i: (bi, i, 0)),
        pl.BlockSpec((1, tm, HEADS * V_DIM), lambda bi, i: (bi, i, 0)),
        _mod_spec(1, 1),
    ] + _param_specs(FFN_PARAMS, FFN_SHAPES, None)
    args = [x, ypg, ymla, mod] + [ffn[n] for n in FFN_PARAMS]
    if final:
        names.append("g_final")
        in_specs.append(pl.BlockSpec((1, D), lambda bi, i: (0, 0)))
        args.append(g_final)
    names.append("o")
    return pl.pallas_call(
        functools.partial(_post_kernel, names=tuple(names), final=final),
        grid=(b, seq // tm),
        in_specs=in_specs,
        out_specs=pl.BlockSpec((1, tm, D), lambda bi, i: (bi, i, 0)),
        out_shape=jax.ShapeDtypeStruct((b, seq, D), F32),
        compiler_params=_params(56),
        name="post_lat",
    )(*args)


def _rope_tables(seq):
    rows = seq // GRID_W
    row = jnp.repeat(jnp.arange(rows, dtype=F32), GRID_W)
    col = jnp.tile(jnp.arange(GRID_W, dtype=F32), rows)
    n_freq = QK_ROPE // 4
    inv = 1.0 / (ROPE_THETA ** (jnp.arange(n_freq, dtype=F32) / n_freq))
    ang = jnp.concatenate([row[:, None] * inv, col[:, None] * inv], axis=-1)
    cos, sin = jnp.cos(ang), jnp.sin(ang)
    z = jnp.zeros_like(cos)
    c = jnp.concatenate([cos, cos, cos, cos], axis=-1)
    s1 = jnp.concatenate([-sin, z, -sin, z], axis=-1)
    s2 = jnp.concatenate([z, sin, z, sin], axis=-1)
    return c, s1, s2


def kernel(x_prompt, x_sample, cache_ckv, cache_krope, c, c_ctx, w_ada, b_ada, g_mix, w_in, w_pool,
           pool_scale, g_q, w_uq, g_kv, w_ukv, g_sgu, w_s, b_s, w_out, g_ffn, w_ff1, w_ff2, g_final):
    dec_b = x_sample.shape[0]
    w_in_p, w_in_g = _win_prep(w_in)
    w_uq_h = w_uq.astype(BF16).reshape(DEPTH, Q_RANK, HEADS // 2, 2, QK_NOPE + QK_ROPE)
    w_uq_p = jnp.concatenate([w_uq_h[..., 0, :QK_NOPE], w_uq_h[..., 1, :QK_NOPE],
                              w_uq_h[..., 0, QK_NOPE:], w_uq_h[..., 1, QK_NOPE:]], axis=-1)
    w_uq_p = w_uq_p.reshape(DEPTH, Q_RANK, HEADS // 2 * Q_PAIR)
    w_ukv_h = w_ukv.astype(BF16).reshape(DEPTH, KV_RANK, HEADS, QK_NOPE + V_DIM)
    w_uk = w_ukv_h[..., :QK_NOPE].reshape(DEPTH, KV_RANK, HEADS * QK_NOPE)
    w_uv = w_ukv_h[..., QK_NOPE:].reshape(DEPTH, KV_RANK, HEADS * V_DIM)
    w_pool_bd = jnp.zeros((DEPTH, POOL_W, POOL_W), BF16)
    for g in range(len(POOL_WINDOWS)):
        sl = slice(g * POOL_GD, (g + 1) * POOL_GD)
        w_pool_bd = w_pool_bd.at[:, sl, sl].set(w_pool[:, g].astype(BF16))
    weights = {
        "g_mix": g_mix.reshape(DEPTH, 1, D),
        "w_in": w_in_p,
        "w_in_g": w_in_g,
        "g_q": (g_q * SM_SCALE).reshape(DEPTH, 1, Q_RANK),
        "w_uq": w_uq_p,
        "g_kv": g_kv.reshape(DEPTH, 1, KV_RANK),
        "w_uk": w_uk,
        "w_uv": w_uv,
        "w_pool": w_pool_bd,
        "pool_scale": pool_scale.reshape(DEPTH, 1, POOL_W),
        "g_sgu": g_sgu.reshape(DEPTH, 1, GMLP_W),
        "w_s": w_s.reshape(DEPTH, GMLP_G * CHUNK, CHUNK).astype(BF16),
        "b_s": jnp.repeat(jnp.swapaxes(b_s, 1, 2), GMLP_GD, axis=2),
    }
    g_fin = g_final.reshape(1, D)
    rope_tabs = _rope_tables(x_sample.shape[1])

    n_rows = 16
    c_all = jnp.zeros((n_rows, D), F32).at[0].set(c_ctx).at[1:1 + dec_b].set(c)
    b_ada3 = b_ada.reshape(DEPTH, 1, 6 * D)

    cache_kr_pad = jnp.pad(cache_krope, ((0, 0), (0, 0), (0, 0), (0, 128 - QK_ROPE)))
    cache = _cache_expand(cache_ckv, cache_kr_pad, w_uk, w_uv)

    mod = _ada_first(c_all, w_ada, b_ada3).reshape(n_rows, 1, 6 * D)
    ffn = {"w_out": w_out[0].astype(BF16), "w_ff1": w_ff1[0].astype(BF16), "w_ff2": w_ff2[0].astype(BF16)}
    f32_params = {"c_all": c_all, "w_ada": w_ada, "b_ada": b_ada3, "w_out": w_out, "w_ff1": w_ff1, "w_ff2": w_ff2}

    xp, xs = x_prompt, x_sample
    ckv_list, kr_list = [], []
    for l in range(DEPTH):
        last = l == DEPTH - 1
        g_last = g_fin if last else None
        ffn["g_ffn"] = g_ffn[l].reshape(1, D)
        xp, sckv, skr = _ctx_layer(xp, mod, l, weights, ffn, g_last)
        ckv_list.append(sckv)
        kr_list.append(skr)
        q, k, v, ypg = _pre(xs, mod, l, weights, rope_tabs, tm=1024)
        outs = _attn(q, k, v, cache, l, None if last else f32_params, tq=1024)
        xs = _post(xs, ypg, outs[0], mod, ffn, g_last, tm=1024)
        if not last:
            mod = outs[1].reshape(n_rows, 1, 6 * D)
            ffn = dict(zip(CAST_PARAMS, outs[2:]))
    return xp, xs, jnp.stack(ckv_list, axis=1), jnp.stack(kr_list, axis=1)
```

```python
import functools
import math

import jax
import jax.numpy as jnp
from jax.experimental import pallas as pl
from jax.experimental.pallas import tpu as pltpu

D = 1024
DEPTH = 4
GRID_W = 64
EPS = 1e-6
POOL_W = 256
POOL_WINDOWS = (2, 4, 8, 16)
POOL_GD = 64
HALO = 8
QK_NOPE = 128
QK_ROPE = 64
V_DIM = 128
V_PAD = 2 * V_DIM
HEADS = 4
Q_RANK = 384
KV_RANK = 256
ROPE_THETA = 10000.0
CHUNK = 128
GMLP_W = 256
GMLP_G = 4
GMLP_GD = 64
D_FF = 4096
FF_CHUNK = 1024
P_Q = 256
P_KV = P_Q + Q_RANK
P_R = P_KV + KV_RANK
P_G = P_R + 128
QK_PAD = 256
Q_PAIR = 2 * QK_NOPE + 2 * QK_ROPE
Q_UNIT = 512
CTX_GROUP = 2
LOOKAHEAD = 1
SM_SCALE = math.log2(math.e) / math.sqrt(QK_NOPE + QK_ROPE)
NT_DIMS = (((1,), (1,)), ((), ()))

F32 = jnp.float32
BF16 = jnp.bfloat16
MIB = 1024 * 1024

MIXER_PARAMS = ("g_mix", "w_in", "w_in_g", "g_q", "w_uq", "g_kv", "w_uk", "w_uv",
                "w_pool", "pool_scale", "g_sgu", "w_s", "b_s")
MIXER_SHAPES = {
    "g_mix": (1, D), "w_in": (D, P_G), "w_in_g": (D, 2 * GMLP_W), "g_q": (1, Q_RANK),
    "w_uq": (Q_RANK, HEADS // 2 * Q_PAIR), "g_kv": (1, KV_RANK), "w_uk": (KV_RANK, HEADS * QK_NOPE),
    "w_uv": (KV_RANK, HEADS * V_DIM), "w_pool": (POOL_W, POOL_W), "pool_scale": (1, POOL_W),
    "g_sgu": (1, GMLP_W), "w_s": (GMLP_G * CHUNK, CHUNK), "b_s": (CHUNK, GMLP_W),
}
FFN_PARAMS = ("w_out", "g_ffn", "w_ff1", "w_ff2")
FFN_SHAPES = {"w_out": (D, D), "g_ffn": (1, D), "w_ff1": (D, D_FF), "w_ff2": (D_FF, D)}
SINGLE_BUFFERED = ("w_in", "w_out", "w_ff1", "w_ff2")


def _params(vmem_mib):
    return pltpu.CompilerParams(
        dimension_semantics=("arbitrary", "arbitrary"),
        vmem_limit_bytes=vmem_mib * MIB,
    )


def _rms(x, g):
    y = x * jax.lax.rsqrt(jnp.mean(x * x, axis=-1, keepdims=True) + EPS)
    return y * g


def _ada_rms(x, g, scale, shift):
    y = x * jax.lax.rsqrt(jnp.mean(x * x, axis=-1, keepdims=True) + EPS)
    return y * (g * (1.0 + scale)) + shift


def _dot(a, b):
    return jnp.dot(a, b, preferred_element_type=F32)


def _dot_nt(a, b):
    return jax.lax.dot_general(a, b, NT_DIMS, preferred_element_type=F32)


def _slab_spec(shape, layer, single=False):
    if layer is None:
        block, index_map = shape, (lambda bi, i: (0,) * len(shape))
    else:
        block, index_map = (None,) + shape, (lambda bi, i: (layer,) + (0,) * len(shape))
    if single:
        return pl.BlockSpec(block, index_map, pipeline_mode=pl.Buffered(1))
    return pl.BlockSpec(block, index_map)


def _param_specs(names, shapes, layer):
    return [_slab_spec(shapes[n], layer, n in SINGLE_BUFFERED) for n in names]


def _mod_spec(row0, per_batch):
    return pl.BlockSpec((None, 1, 6 * D), lambda bi, i: (bi * per_batch + row0, 0, 0))


CAST_PARAMS = ("w_out", "w_ff1", "w_ff2")


def _cast_slabs(src, layer, nt, steps):
    side = {"names": [], "in_specs": [], "args": [], "out_names": [], "out_specs": [], "out_shape": []}
    for n in CAST_PARAMS:
        k_dim, n_dim = FFN_SHAPES[n]
        tr = k_dim // steps
        side["names"].append(n + "_f32")
        side["in_specs"].append(pl.BlockSpec((None, tr, n_dim), lambda bi, i: (layer, bi * nt + i, 0)))
        side["args"].append(src[n])
        side["out_names"].append(n + "_bf16")
        side["out_specs"].append(pl.BlockSpec((tr, n_dim), lambda bi, i: (bi * nt + i, 0)))
        side["out_shape"].append(jax.ShapeDtypeStruct((k_dim, n_dim), BF16))
    return side


def _cast_slabs_body(r):
    for n in CAST_PARAMS:
        r[n + "_bf16"][...] = r[n + "_f32"][...].astype(BF16)


def _modulation(c, w, b):
    s = c / (1.0 + jnp.exp(-c))
    return _dot(s.astype(BF16), w.astype(BF16)) + b


def _ada_kernel(c_ref, w_ref, b_ref, o_ref):
    o_ref[...] = _modulation(c_ref[...], w_ref[...], b_ref[...])


def _ada_first(c_all, w_ada, b_ada):
    rows = c_all.shape[0]
    tn = 1536
    return pl.pallas_call(
        _ada_kernel,
        grid=(1, 6 * D // tn),
        in_specs=[
            pl.BlockSpec((rows, D), lambda l, j: (0, 0)),
            pl.BlockSpec((None, D, tn), lambda l, j: (0, 0, j)),
            pl.BlockSpec((None, 1, tn), lambda l, j: (0, 0, j)),
        ],
        out_specs=pl.BlockSpec((rows, tn), lambda l, j: (0, j)),
        out_shape=jax.ShapeDtypeStruct((rows, 6 * D), F32),
        compiler_params=_params(32),
        name="ada",
    )(c_all, w_ada, b_ada)


def _win_prep_kernel(w_ref, wa_ref, wg_ref):
    lane = jax.lax.broadcasted_iota(jnp.int32, (1, P_G - P_R), 1)
    wa_ref[:, 0:P_R] = w_ref[:, 0:P_R].astype(BF16)
    wa_ref[:, P_R:P_G] = jnp.where(lane < QK_ROPE, w_ref[:, P_R:P_G], 0.0).astype(BF16)
    wg_ref[...] = w_ref[:, P_R + QK_ROPE:].astype(BF16)


def _win_prep(w_in):
    cols = w_in.shape[2]
    return pl.pallas_call(
        _win_prep_kernel,
        grid=(DEPTH,),
        in_specs=[pl.BlockSpec((None, D, cols), lambda l: (l, 0, 0))],
        out_specs=[
            pl.BlockSpec((None, D, P_G), lambda l: (l, 0, 0)),
            pl.BlockSpec((None, D, 2 * GMLP_W), lambda l: (l, 0, 0)),
        ],
        out_shape=[
            jax.ShapeDtypeStruct((DEPTH, D, P_G), BF16),
            jax.ShapeDtypeStruct((DEPTH, D, 2 * GMLP_W), BF16),
        ],
        compiler_params=pltpu.CompilerParams(dimension_semantics=("arbitrary",), vmem_limit_bytes=32 * MIB),
        name="win_prep",
    )(w_in)


def _cache_kernel(ckv_ref, kr_ref, wk_ref, wv_ref, k_ref, v_ref):
    for s in range(ckv_ref.shape[0]):
        ckv = ckv_ref[s, 0].astype(BF16)
        kn = _dot(ckv, wk_ref[...])
        v = _dot(ckv, wv_ref[...])
        ones = jnp.ones((v.shape[0], V_PAD - V_DIM), BF16)
        kr = kr_ref[s, 0]
        kr_tiles = (kr.astype(BF16), pltpu.roll(kr, QK_ROPE, axis=1).astype(BF16))
        for h in range(HEADS):
            k_ref[s, 0, h] = jnp.concatenate(
                [kn[:, h * QK_NOPE:(h + 1) * QK_NOPE].astype(BF16), kr_tiles[h % 2]], axis=1)
            v_ref[s, 0, h] = jnp.concatenate([v[:, h * V_DIM:(h + 1) * V_DIM].astype(BF16), ones], axis=1)


def _cache_expand(cache_ckv, cache_kr_pad, w_uk, w_uv):
    b, depth, p, _ = cache_ckv.shape
    group = 4
    return pl.pallas_call(
        _cache_kernel,
        grid=(depth, b // group),
        in_specs=[
            pl.BlockSpec((group, 1, p, KV_RANK), lambda l, i: (i, l, 0, 0)),
            pl.BlockSpec((group, 1, p, 128), lambda l, i: (i, l, 0, 0)),
            pl.BlockSpec((None, KV_RANK, HEADS * QK_NOPE), lambda l, i: (l, 0, 0)),
            pl.BlockSpec((None, KV_RANK, HEADS * V_DIM), lambda l, i: (l, 0, 0)),
        ],
        out_specs=[
            pl.BlockSpec((group, 1, HEADS, p, QK_PAD), lambda l, i: (i, l, 0, 0, 0)),
            pl.BlockSpec((group, 1, HEADS, p, V_PAD), lambda l, i: (i, l, 0, 0, 0)),
        ],
        out_shape=[
            jax.ShapeDtypeStruct((b, depth, HEADS, p, QK_PAD), BF16),
            jax.ShapeDtypeStruct((b, depth, HEADS, p, V_PAD), BF16),
        ],
        compiler_params=_params(32),
        name="cache_expand",
    )(cache_ckv, cache_kr_pad, w_uk, w_uv)


def _rows_up(a, k):
    return pltpu.roll(a, a.shape[0] - k, axis=0)


def _rows_down(a, k):
    return pltpu.roll(a, k, axis=0)


def _rope(x, c, s1, s2):
    return x * c + pltpu.roll(x, 96, axis=1) * s1 + pltpu.roll(x, 32, axis=1) * s2


def _pool_mix(hp_e, hp, t0, seq, r):
    tm = hp.shape[0]
    main = slice(HALO, HALO + tm)
    lane_grp = jax.lax.broadcasted_iota(jnp.int32, (1, 2 * POOL_GD), 1) // POOL_GD
    xa = hp_e[:, 0:2 * POOL_GD]
    s2 = xa + _rows_down(xa, 1)
    s4 = _rows_up(s2, 1) + _rows_down(s2, 1)
    win_a = jnp.where(lane_grp == 0, s2[main], s4[main])
    xb = hp_e[:, 2 * POOL_GD:POOL_W]
    f2 = xb + _rows_up(xb, 1)
    f4 = f2 + _rows_up(f2, 2)
    f8 = f4 + _rows_up(f4, 4)
    s8 = _rows_down(f8, 4)
    s16 = f8[0:tm] + f8[main]
    win_b = jnp.where(lane_grp == 0, s8[main], s16)
    win = jnp.concatenate([win_a, win_b], axis=1)
    grp = jax.lax.broadcasted_iota(jnp.int32, (1, POOL_W), 1) // POOL_GD
    half = jnp.where(grp == 0, 1, jnp.where(grp == 1, 2, jnp.where(grp == 2, 4, 8)))
    t = t0 + jax.lax.broadcasted_iota(jnp.int32, (tm, 1), 0)
    cnt = jnp.clip(t + half, 0, seq) - jnp.clip(t - half, 0, seq)
    pooled = win / cnt.astype(F32) - hp
    return _dot(pooled.astype(BF16), r["w_pool"][...]) * r["pool_scale"][...]


def _gmlp(proj_g, r):
    tm = proj_g.shape[0]
    uv = jax.nn.gelu(proj_g)
    u = uv[:, 0:GMLP_W]
    vg = _rms(uv[:, GMLP_W:], r["g_sgu"][...]).astype(BF16)
    ggrp = jax.lax.broadcasted_iota(jnp.int32, (1, GMLP_W), 1) // GMLP_GD
    ys = []
    for cix in range(tm // CHUNK):
        m = _dot(r["w_s"][...], vg[cix * CHUNK:(cix + 1) * CHUNK])
        mixed = jnp.where(
            ggrp == 0, m[0:CHUNK],
            jnp.where(ggrp == 1, m[CHUNK:2 * CHUNK],
                      jnp.where(ggrp == 2, m[2 * CHUNK:3 * CHUNK], m[3 * CHUNK:4 * CHUNK])))
        ys.append(u[cix * CHUNK:(cix + 1) * CHUNK] * (mixed + r["b_s"][...]))
    return jnp.concatenate(ys, axis=0)


def _mla_operands(proj, r, rope):
    cq = _rms(proj[:, P_Q:P_KV], r["g_q"][...])
    q = _dot(cq.astype(BF16), r["w_uq"][...])
    ckv = _rms(proj[:, P_KV:P_R], r["g_kv"][...])
    ckv_b = ckv.astype(BF16)
    kn = _dot(ckv_b, r["w_uk"][...])
    v = _dot(ckv_b, r["w_uv"][...])
    ones = jnp.ones((v.shape[0], V_PAD - V_DIM), BF16)
    kr_raw = proj[:, P_R:P_G]
    kr = kr_raw if rope is None else _rope(kr_raw, *rope)
    kr_tiles = (kr.astype(BF16), pltpu.roll(kr, QK_ROPE, axis=1).astype(BF16))
    qs, ks, vs = [], [], []
    for pair in range(HEADS // 2):
        base = pair * Q_PAIR
        qr = q[:, base + 2 * QK_NOPE:base + Q_PAIR]
        if rope is not None:
            qr = _rope(qr, *rope)
        qr_b = qr.astype(BF16)
        for j in range(2):
            hd = 2 * pair + j
            qn = q[:, base + j * QK_NOPE:base + (j + 1) * QK_NOPE]
            qs.append(jnp.concatenate([qn.astype(BF16), qr_b], axis=1))
            ks.append(jnp.concatenate([kn[:, hd * QK_NOPE:(hd + 1) * QK_NOPE].astype(BF16), kr_tiles[j]], axis=1))
            vs.append(jnp.concatenate([v[:, hd * V_DIM:(hd + 1) * V_DIM].astype(BF16), ones], axis=1))
    return qs, ks, vs, ckv, kr_raw


def _softmax_pv(s, sc, v, vc):
    m = jnp.max(s, axis=1, keepdims=True)
    if sc is not None:
        m = jnp.maximum(m, jnp.max(sc, axis=1, keepdims=True))
    oa = _dot(jnp.exp2(s - m).astype(BF16), v)
    if sc is not None:
        oa = oa + _dot(jnp.exp2(sc - m).astype(BF16), vc)
    return oa[:, 0:V_DIM] / oa[:, V_DIM:V_DIM + 1]


def _ffn_block(x, ycat, mod_ref, r, g_final):
    g1 = mod_ref[:, 2 * D:3 * D]
    sh2 = mod_ref[:, 3 * D:4 * D]
    sc2 = mod_ref[:, 4 * D:5 * D]
    g2 = mod_ref[:, 5 * D:6 * D]
    x1 = x + g1 * _dot(ycat, r["w_out"][...])
    h = _ada_rms(x1, r["g_ffn"][...], sc2, sh2).astype(BF16)
    acc = None
    for j in range(D_FF // FF_CHUNK):
        a = _dot(h, r["w_ff1"][:, j * FF_CHUNK:(j + 1) * FF_CHUNK])
        a = jnp.square(jnp.maximum(a, 0.0)).astype(BF16)
        part = _dot(a, r["w_ff2"][j * FF_CHUNK:(j + 1) * FF_CHUNK, :])
        acc = part if acc is None else acc + part
    out = x1 + g2 * acc
    if g_final is not None:
        out = _rms(out, g_final)
    return out


def _ctx_kernel(*refs, names, final):
    r = dict(zip(names, refs))
    mod_ref = r["mod"]
    group, seq, _ = r["x"].shape
    x = r["x"][...].reshape(group * seq, D)
    h = _ada_rms(x, r["g_mix"][...], mod_ref[:, D:2 * D], mod_ref[:, 0:D]).astype(BF16)
    proj = _dot(h, r["w_in"][...])
    proj_g = _dot(h, r["w_in_g"][...])
    y_g = _gmlp(proj_g, r)
    qs, ks, vs, ckv, kr_raw = _mla_operands(proj, r, None)
    r["sckv"][...] = ckv.reshape(group, seq, KV_RANK)
    r["skr"][...] = kr_raw[:, 0:QK_ROPE].reshape(group, seq, QK_ROPE)
    edge = jnp.zeros((HALO, POOL_W), F32)
    ycats = []
    for s in range(group):
        own = slice(s * seq, (s + 1) * seq)
        hp = proj[own, 0:POOL_W]
        y_pool = _pool_mix(jnp.concatenate([edge, hp, edge], axis=0), hp, 0, seq, r)
        heads = [_softmax_pv(_dot_nt(qs[hd][own], ks[hd][own]), None, vs[hd][own], None)
                 for hd in range(HEADS)]
        ycats.append(jnp.concatenate([y_pool] + heads + [y_g[own]], axis=1).astype(BF16))
    out = _ffn_block(x, jnp.concatenate(ycats, axis=0), mod_ref, r, r["g_final"][...] if final else None)
    r["o"][...] = out.reshape(group, seq, D)


def _ctx_layer(x, mod, layer, w, ffn, g_final):
    b, seq, _ = x.shape
    final = g_final is not None
    names = ["x", "mod"] + list(MIXER_PARAMS) + list(FFN_PARAMS)
    in_specs = ([pl.BlockSpec((CTX_GROUP, seq, D), lambda bi, i: (bi, 0, 0)), _mod_spec(0, 0)]
                + _param_specs(MIXER_PARAMS, MIXER_SHAPES, layer)
                + _param_specs(FFN_PARAMS, FFN_SHAPES, None))
    args = [x, mod] + [w[n] for n in MIXER_PARAMS] + [ffn[n] for n in FFN_PARAMS]
    if final:
        names.append("g_final")
        in_specs.append(pl.BlockSpec((1, D), lambda bi, i: (0, 0)))
        args.append(g_final)
    names += ["o", "sckv", "skr"]
    return pl.pallas_call(
        functools.partial(_ctx_kernel, names=tuple(names), final=final),
        grid=(b // CTX_GROUP, 1),
        in_specs=in_specs,
        out_specs=[
            pl.BlockSpec((CTX_GROUP, seq, D), lambda bi, i: (bi, 0, 0)),
            pl.BlockSpec((CTX_GROUP, seq, KV_RANK), lambda bi, i: (bi, 0, 0)),
            pl.BlockSpec((CTX_GROUP, seq, QK_ROPE), lambda bi, i: (bi, 0, 0)),
        ],
        out_shape=[
            jax.ShapeDtypeStruct((b, seq, D), F32),
            jax.ShapeDtypeStruct((b, seq, KV_RANK), F32),
            jax.ShapeDtypeStruct((b, seq, QK_ROPE), F32),
        ],
        compiler_params=_params(52),
        name="ctx_layer",
    )(*args)


def _pre_kernel(*refs, names, tm, seq, cast):
    r = dict(zip(names, refs))
    mod_ref = r["mod"]
    i = pl.program_id(1)
    rows = tm + 2 * HALO
    xe = jnp.concatenate([r["x_prev"][0], r["x"][0], r["x_next"][0]], axis=0)
    h = _ada_rms(xe, r["g_mix"][...], mod_ref[:, D:2 * D], mod_ref[:, 0:D]).astype(BF16)
    proj_e = _dot(h, r["w_in"][...])
    proj = proj_e[HALO:HALO + tm]
    proj_g = _dot(h, r["w_in_g"][...])[HALO:HALO + tm]

    pos = i * tm - HALO + jax.lax.broadcasted_iota(jnp.int32, (rows, 1), 0)
    hp_e = jnp.where((pos >= 0) & (pos < seq), proj_e[:, 0:POOL_W], 0.0)
    y_pool = _pool_mix(hp_e, proj[:, 0:POOL_W], i * tm, seq, r)

    rope = (r["rope_c"][...], r["rope_s1"][...], r["rope_s2"][...])
    qs, ks, vs, _, _ = _mla_operands(proj, r, rope)
    for hd in range(HEADS):
        r["q"][0, hd] = qs[hd]
        r["k"][0, hd] = ks[hd]
        r["v"][0, hd] = vs[hd]

    y_g = _gmlp(proj_g, r)
    r["ypg"][0] = jnp.concatenate([y_pool, y_g], axis=1).astype(BF16)
    if cast:
        _cast_slabs_body(r)


def _pre(x, mod, layer, w, rope_tabs, cast_src, *, tm):
    b, seq, _ = x.shape
    nt = seq // tm
    hb = tm // HALO
    nhb = seq // HALO
    names = ["x", "x_prev", "x_next", "mod"] + list(MIXER_PARAMS) + ["rope_c", "rope_s1", "rope_s2"]
    out_names = ["q", "k", "v", "ypg"]
    in_specs = ([
        pl.BlockSpec((1, tm, D), lambda bi, i: (bi, i, 0)),
        pl.BlockSpec((1, HALO, D), lambda bi, i: (bi, jnp.maximum(i * hb - 1, 0), 0)),
        pl.BlockSpec((1, HALO, D), lambda bi, i: (bi, jnp.minimum((i + 1) * hb, nhb - 1), 0)),
        _mod_spec(1, 1),
    ] + _param_specs(MIXER_PARAMS, MIXER_SHAPES, layer)
      + [pl.BlockSpec((tm, 128), lambda bi, i: (i, 0))] * 3)
    args = [x, x, x, mod] + [w[n] for n in MIXER_PARAMS] + list(rope_tabs)
    out_specs = [
        pl.BlockSpec((1, HEADS, tm, QK_PAD), lambda bi, i: (bi, 0, i, 0)),
        pl.BlockSpec((1, HEADS, tm, QK_PAD), lambda bi, i: (bi, 0, i, 0)),
        pl.BlockSpec((1, HEADS, tm, V_PAD), lambda bi, i: (bi, 0, i, 0)),
        pl.BlockSpec((1, tm, 2 * GMLP_W), lambda bi, i: (bi, i, 0)),
    ]
    out_shape = [
        jax.ShapeDtypeStruct((b, HEADS, seq, QK_PAD), BF16),
        jax.ShapeDtypeStruct((b, HEADS, seq, QK_PAD), BF16),
        jax.ShapeDtypeStruct((b, HEADS, seq, V_PAD), BF16),
        jax.ShapeDtypeStruct((b, seq, 2 * GMLP_W), BF16),
    ]
    if cast_src is not None:
        side = _cast_slabs(cast_src, layer, nt, b * nt)
        names, in_specs, args = names + side["names"], in_specs + side["in_specs"], args + side["args"]
        out_names, out_specs = out_names + side["out_names"], out_specs + side["out_specs"]
        out_shape = out_shape + side["out_shape"]
    return pl.pallas_call(
        functools.partial(_pre_kernel, names=tuple(names + out_names), tm=tm, seq=seq,
                          cast=cast_src is not None),
        grid=(b, nt),
        in_specs=in_specs,
        out_specs=out_specs,
        out_shape=out_shape,
        compiler_params=_params(56),
        name="pre_lat",
    )(*args)


def _attn_kernel(*refs, names, tq, prep_next):
    r = dict(zip(names, refs))
    q_ref, k_ref, v_ref, kc_ref, vc_ref, o_ref = (r[n] for n in ("q", "k", "v", "kc", "vc", "o"))
    units = [(h, slice(u * Q_UNIT, (u + 1) * Q_UNIT)) for h in range(HEADS) for u in range(tq // Q_UNIT)]

    def scores(unit):
        h, rows = unit
        q = q_ref[0, h, rows, :]
        return _dot_nt(q, k_ref[0, h]), _dot_nt(q, kc_ref[0, 0, h])

    ahead = [scores(u) for u in units[:LOOKAHEAD]]
    for idx, (h, rows) in enumerate(units):
        s, sc = ahead.pop(0)
        if idx + LOOKAHEAD < len(units):
            ahead.append(scores(units[idx + LOOKAHEAD]))
        o = _softmax_pv(s, sc, v_ref[0, h], vc_ref[0, 0, h])
        o_ref[0, rows, h * V_DIM:(h + 1) * V_DIM] = o.astype(BF16)

    if prep_next:
        r["mod_next"][...] = _modulation(r["c_all"][...], r["w_ada"][...], r["b_ada"][...])
        _cast_slabs_body(r)


def _attn(q, k, v, cache, layer, nxt, *, tq):
    b, _, seq, _ = q.shape
    nt = seq // tq
    steps = b * nt
    kc, vc = cache
    p = kc.shape[3]
    names = ["q", "k", "v", "kc", "vc"]
    in_specs = [
        pl.BlockSpec((1, HEADS, tq, QK_PAD), lambda bi, i: (bi, 0, i, 0)),
        pl.BlockSpec((1, HEADS, seq, QK_PAD), lambda bi, i: (bi, 0, 0, 0)),
        pl.BlockSpec((1, HEADS, seq, V_PAD), lambda bi, i: (bi, 0, 0, 0)),
        pl.BlockSpec((1, 1, HEADS, p, QK_PAD), lambda bi, i: (bi, layer, 0, 0, 0)),
        pl.BlockSpec((1, 1, HEADS, p, V_PAD), lambda bi, i: (bi, layer, 0, 0, 0)),
    ]
    args = [q, k, v, kc, vc]
    out_names = ["o"]
    out_specs = [pl.BlockSpec((1, tq, HEADS * V_DIM), lambda bi, i: (bi, i, 0))]
    out_shape = [jax.ShapeDtypeStruct((b, seq, HEADS * V_DIM), BF16)]
    if nxt is not None:
        rows = nxt["c_all"].shape[0]
        tn = 6 * D // steps
        names += ["c_all", "w_ada", "b_ada"]
        in_specs += [
            pl.BlockSpec((rows, D), lambda bi, i: (0, 0)),
            pl.BlockSpec((None, D, tn), lambda bi, i: (layer + 1, 0, bi * nt + i)),
            pl.BlockSpec((None, 1, tn), lambda bi, i: (layer + 1, 0, bi * nt + i)),
        ]
        args += [nxt["c_all"], nxt["w_ada"], nxt["b_ada"]]
        out_names.append("mod_next")
        out_specs.append(pl.BlockSpec((rows, tn), lambda bi, i: (0, bi * nt + i)))
        out_shape.append(jax.ShapeDtypeStruct((rows, 6 * D), F32))
        side = _cast_slabs(nxt, layer + 1, nt, steps)
        names, in_specs, args = names + side["names"], in_specs + side["in_specs"], args + side["args"]
        out_names, out_specs = out_names + side["out_names"], out_specs + side["out_specs"]
        out_shape = out_shape + side["out_shape"]
    return pl.pallas_call(
        functools.partial(_attn_kernel, names=tuple(names + out_names), tq=tq, prep_next=nxt is not None),
        grid=(b, nt),
        in_specs=in_specs,
        out_specs=out_specs,
        out_shape=out_shape,
        compiler_params=_params(52),
        name="attn_lat",
    )(*args)


def _post_kernel(*refs, names, final):
    r = dict(zip(names, refs))
    ypg = r["ypg"][0]
    ycat = jnp.concatenate([ypg[:, 0:POOL_W], r["ymla"][0], ypg[:, POOL_W:]], axis=1)
    r["o"][0] = _ffn_block(r["x"][0], ycat, r["mod"], r, r["g_final"][...] if final else None)


def _post(x, ypg, ymla, mod, ffn, g_final, *, tm):
    b, seq, _ = x.shape
    final = g_final is not None
    names = ["x", "ypg", "ymla", "mod"] + list(FFN_PARAMS)
    in_specs = [
        pl.BlockSpec((1, tm, D), lambda bi, i: (bi, i, 0)),
        pl.BlockSpec((1, tm, 2 * GMLP_W), lambda bi, i: (bi, i, 0)),
        pl.BlockSpec((1, tm, HEADS * V_DIM), lambda bi, i: (bi, i, 0)),
        _mod_spec(1, 1),
    ] + _param_specs(FFN_PARAMS, FFN_SHAPES, None)
    args = [x, ypg, ymla, mod] + [ffn[n] for n in FFN_PARAMS]
    if final:
        names.append("g_final")
        in_specs.append(pl.BlockSpec((1, D), lambda bi, i: (0, 0)))
        args.append(g_final)
    names.append("o")
    return pl.pallas_call(
        functools.partial(_post_kernel, names=tuple(names), final=final),
        grid=(b, seq // tm),
        in_specs=in_specs,
        out_specs=pl.BlockSpec((1, tm, D), lambda bi, i: (bi, i, 0)),
        out_shape=jax.ShapeDtypeStruct((b, seq, D), F32),
        compiler_params=_params(56),
        name="post_lat",
    )(*args)


def _rope_tables(seq):
    rows = seq // GRID_W
    row = jnp.repeat(jnp.arange(rows, dtype=F32), GRID_W)
    col = jnp.tile(jnp.arange(GRID_W, dtype=F32), rows)
    n_freq = QK_ROPE // 4
    inv = 1.0 / (ROPE_THETA ** (jnp.arange(n_freq, dtype=F32) / n_freq))
    ang = jnp.concatenate([row[:, None] * inv, col[:, None] * inv], axis=-1)
    cos, sin = jnp.cos(ang), jnp.sin(ang)
    z = jnp.zeros_like(cos)
    c = jnp.concatenate([cos, cos, cos, cos], axis=-1)
    s1 = jnp.concatenate([-sin, z, -sin, z], axis=-1)
    s2 = jnp.concatenate([z, sin, z, sin], axis=-1)
    return c, s1, s2


def kernel(x_prompt, x_sample, cache_ckv, cache_krope, c, c_ctx, w_ada, b_ada, g_mix, w_in, w_pool,
           pool_scale, g_q, w_uq, g_kv, w_ukv, g_sgu, w_s, b_s, w_out, g_ffn, w_ff1, w_ff2, g_final):
    dec_b = x_sample.shape[0]
    w_in_p, w_in_g = _win_prep(w_in)
    w_uq_h = w_uq.astype(BF16).reshape(DEPTH, Q_RANK, HEADS // 2, 2, QK_NOPE + QK_ROPE)
    w_uq_p = jnp.concatenate([w_uq_h[..., 0, :QK_NOPE], w_uq_h[..., 1, :QK_NOPE],
                              w_uq_h[..., 0, QK_NOPE:], w_uq_h[..., 1, QK_NOPE:]], axis=-1)
    w_uq_p = w_uq_p.reshape(DEPTH, Q_RANK, HEADS // 2 * Q_PAIR)
    w_ukv_h = w_ukv.astype(BF16).reshape(DEPTH, KV_RANK, HEADS, QK_NOPE + V_DIM)
    w_uk = w_ukv_h[..., :QK_NOPE].reshape(DEPTH, KV_RANK, HEADS * QK_NOPE)
    w_uv = w_ukv_h[..., QK_NOPE:].reshape(DEPTH, KV_RANK, HEADS * V_DIM)
    w_pool_bd = jnp.zeros((DEPTH, POOL_W, POOL_W), BF16)
    for g in range(len(POOL_WINDOWS)):
        sl = slice(g * POOL_GD, (g + 1) * POOL_GD)
        w_pool_bd = w_pool_bd.at[:, sl, sl].set(w_pool[:, g].astype(BF16))
    weights = {
        "g_mix": g_mix.reshape(DEPTH, 1, D),
        "w_in": w_in_p,
        "w_in_g": w_in_g,
        "g_q": (g_q * SM_SCALE).reshape(DEPTH, 1, Q_RANK),
        "w_uq": w_uq_p,
        "g_kv": g_kv.reshape(DEPTH, 1, KV_RANK),
        "w_uk": w_uk,
        "w_uv": w_uv,
        "w_pool": w_pool_bd,
        "pool_scale": pool_scale.reshape(DEPTH, 1, POOL_W),
        "g_sgu": g_sgu.reshape(DEPTH, 1, GMLP_W),
        "w_s": w_s.reshape(DEPTH, GMLP_G * CHUNK, CHUNK).astype(BF16),
        "b_s": jnp.repeat(jnp.swapaxes(b_s, 1, 2), GMLP_GD, axis=2),
    }
    g_fin = g_final.reshape(1, D)
    rope_tabs = _rope_tables(x_sample.shape[1])

    n_rows = 16
    c_all = jnp.zeros((n_rows, D), F32).at[0].set(c_ctx).at[1:1 + dec_b].set(c)
    b_ada3 = b_ada.reshape(DEPTH, 1, 6 * D)

    cache_kr_pad = jnp.pad(cache_krope, ((0, 0), (0, 0), (0, 0), (0, 128 - QK_ROPE)))
    cache = _cache_expand(cache_ckv, cache_kr_pad, w_uk, w_uv)

    mod = _ada_first(c_all, w_ada, b_ada3).reshape(n_rows, 1, 6 * D)
    f32_params = {"c_all": c_all, "w_ada": w_ada, "b_ada": b_ada3, "w_out": w_out, "w_ff1": w_ff1, "w_ff2": w_ff2}

    xp, xs = x_prompt, x_sample
    ckv_list, kr_list = [], []
    for l in range(DEPTH):
        last = l == DEPTH - 1
        g_last = g_fin if last else None
        q, k, v, ypg, *cast = _pre(xs, mod, l, weights, rope_tabs, f32_params if l == 0 else None, tm=1024)
        if l == 0:
            ffn = dict(zip(CAST_PARAMS, cast))
        ffn["g_ffn"] = g_ffn[l].reshape(1, D)
        xp, sckv, skr = _ctx_layer(xp, mod, l, weights, ffn, g_last)
        ckv_list.append(sckv)
        kr_list.append(skr)
        outs = _attn(q, k, v, cache, l, None if last else f32_params, tq=1024)
        xs = _post(xs, ypg, outs[0], mod, ffn, g_last, tm=1024)
        if not last:
            mod = outs[1].reshape(n_rows, 1, 6 * D)
            ffn = dict(zip(CAST_PARAMS, outs[2:]))
    return xp, xs, jnp.stack(ckv_list, axis=1), jnp.stack(kr_list, axis=1)
```

```python
import functools
import math

import jax
import jax.numpy as jnp
from jax.experimental import pallas as pl
from jax.experimental.pallas import tpu as pltpu

D = 1024
DEPTH = 4
GRID_W = 64
EPS = 1e-6
POOL_W = 256
POOL_WINDOWS = (2, 4, 8, 16)
POOL_GD = 64
HALO = 8
QK_NOPE = 128
QK_ROPE = 64
V_DIM = 128
V_PAD = 2 * V_DIM
HEADS = 4
Q_RANK = 384
KV_RANK = 256
ROPE_THETA = 10000.0
CHUNK = 128
GMLP_W = 256
GMLP_G = 4
GMLP_GD = 64
D_FF = 4096
FF_CHUNK = 1024
P_Q = 256
P_KV = P_Q + Q_RANK
P_R = P_KV + KV_RANK
P_G = P_R + 128
QK_PAD = 256
Q_PAIR = 2 * QK_NOPE + 2 * QK_ROPE
Q_UNIT = 512
CTX_GROUP = 2
LOOKAHEAD = 1
SM_SCALE = math.log2(math.e) / math.sqrt(QK_NOPE + QK_ROPE)
NT_DIMS = (((1,), (1,)), ((), ()))

F32 = jnp.float32
BF16 = jnp.bfloat16
MIB = 1024 * 1024

MIXER_PARAMS = ("g_mix", "w_in", "w_in_g", "g_q", "w_uq", "g_kv", "w_uk", "w_uv",
                "w_pool", "pool_scale", "g_sgu", "w_s", "b_s")
MIXER_SHAPES = {
    "g_mix": (1, D), "w_in": (D, P_G), "w_in_g": (D, 2 * GMLP_W), "g_q": (1, Q_RANK),
    "w_uq": (Q_RANK, HEADS // 2 * Q_PAIR), "g_kv": (1, KV_RANK), "w_uk": (KV_RANK, HEADS * QK_NOPE),
    "w_uv": (KV_RANK, HEADS * V_DIM), "w_pool": (POOL_W, POOL_W), "pool_scale": (1, POOL_W),
    "g_sgu": (1, GMLP_W), "w_s": (GMLP_G * CHUNK, CHUNK), "b_s": (CHUNK, GMLP_W),
}
FFN_PARAMS = ("w_out", "g_ffn", "w_ff1", "w_ff2")
FFN_SHAPES = {"w_out": (D, D), "g_ffn": (1, D), "w_ff1": (D, D_FF), "w_ff2": (D_FF, D)}
SINGLE_BUFFERED = ("w_in", "w_out", "w_ff1", "w_ff2")


def _params(vmem_mib):
    return pltpu.CompilerParams(
        dimension_semantics=("arbitrary", "arbitrary"),
        vmem_limit_bytes=vmem_mib * MIB,
    )


def _rms(x, g):
    y = x * jax.lax.rsqrt(jnp.mean(x * x, axis=-1, keepdims=True) + EPS)
    return y * g


def _ada_rms(x, g, scale, shift):
    y = x * jax.lax.rsqrt(jnp.mean(x * x, axis=-1, keepdims=True) + EPS)
    return y * (g * (1.0 + scale)) + shift


def _dot(a, b):
    return jnp.dot(a, b, preferred_element_type=F32)


def _dot_nt(a, b):
    return jax.lax.dot_general(a, b, NT_DIMS, preferred_element_type=F32)


def _slab_spec(shape, layer, single=False):
    if layer is None:
        block, index_map = shape, (lambda bi, i: (0,) * len(shape))
    else:
        block, index_map = (None,) + shape, (lambda bi, i: (layer,) + (0,) * len(shape))
    if single:
        return pl.BlockSpec(block, index_map, pipeline_mode=pl.Buffered(1))
    return pl.BlockSpec(block, index_map)


def _param_specs(names, shapes, layer):
    return [_slab_spec(shapes[n], layer, n in SINGLE_BUFFERED) for n in names]


def _mod_spec(row0, per_batch):
    return pl.BlockSpec((None, 1, 6 * D), lambda bi, i: (bi * per_batch + row0, 0, 0))


CAST_PARAMS = ("w_out", "w_ff1", "w_ff2")


def _cast_slabs(src, layer, nt, steps):
    side = {"names": [], "in_specs": [], "args": [], "out_names": [], "out_specs": [], "out_shape": []}
    for n in CAST_PARAMS:
        k_dim, n_dim = FFN_SHAPES[n]
        tr = k_dim // steps
        side["names"].append(n + "_f32")
        side["in_specs"].append(pl.BlockSpec((None, tr, n_dim), lambda bi, i: (layer, bi * nt + i, 0)))
        side["args"].append(src[n])
        side["out_names"].append(n + "_bf16")
        side["out_specs"].append(pl.BlockSpec((tr, n_dim), lambda bi, i: (bi * nt + i, 0)))
        side["out_shape"].append(jax.ShapeDtypeStruct((k_dim, n_dim), BF16))
    return side


def _cast_slabs_body(r):
    for n in CAST_PARAMS:
        r[n + "_bf16"][...] = r[n + "_f32"][...].astype(BF16)


def _modulation(c, w, b):
    s = c / (1.0 + jnp.exp(-c))
    return _dot(s.astype(BF16), w.astype(BF16)) + b


def _ada_kernel(c_ref, w_ref, b_ref, o_ref):
    o_ref[...] = _modulation(c_ref[...], w_ref[...], b_ref[...])


def _ada_first(c_all, w_ada, b_ada):
    rows = c_all.shape[0]
    tn = 1536
    return pl.pallas_call(
        _ada_kernel,
        grid=(1, 6 * D // tn),
        in_specs=[
            pl.BlockSpec((rows, D), lambda l, j: (0, 0)),
            pl.BlockSpec((None, D, tn), lambda l, j: (0, 0, j)),
            pl.BlockSpec((None, 1, tn), lambda l, j: (0, 0, j)),
        ],
        out_specs=pl.BlockSpec((rows, tn), lambda l, j: (0, j)),
        out_shape=jax.ShapeDtypeStruct((rows, 6 * D), F32),
        compiler_params=_params(32),
        name="ada",
    )(c_all, w_ada, b_ada)


def _win_prep_kernel(w_ref, wa_ref, wg_ref):
    lane = jax.lax.broadcasted_iota(jnp.int32, (1, P_G - P_R), 1)
    wa_ref[:, 0:P_R] = w_ref[:, 0:P_R].astype(BF16)
    wa_ref[:, P_R:P_G] = jnp.where(lane < QK_ROPE, w_ref[:, P_R:P_G], 0.0).astype(BF16)
    wg_ref[...] = w_ref[:, P_R + QK_ROPE:].astype(BF16)


def _win_prep(w_in):
    cols = w_in.shape[2]
    return pl.pallas_call(
        _win_prep_kernel,
        grid=(DEPTH,),
        in_specs=[pl.BlockSpec((None, D, cols), lambda l: (l, 0, 0))],
        out_specs=[
            pl.BlockSpec((None, D, P_G), lambda l: (l, 0, 0)),
            pl.BlockSpec((None, D, 2 * GMLP_W), lambda l: (l, 0, 0)),
        ],
        out_shape=[
            jax.ShapeDtypeStruct((DEPTH, D, P_G), BF16),
            jax.ShapeDtypeStruct((DEPTH, D, 2 * GMLP_W), BF16),
        ],
        compiler_params=pltpu.CompilerParams(dimension_semantics=("arbitrary",), vmem_limit_bytes=32 * MIB),
        name="win_prep",
    )(w_in)


def _cache_kernel(ckv_ref, kr_ref, wk_ref, wv_ref, k_ref, v_ref):
    for s in range(ckv_ref.shape[0]):
        ckv = ckv_ref[s, 0].astype(BF16)
        kn = _dot(ckv, wk_ref[...])
        v = _dot(ckv, wv_ref[...])
        ones = jnp.ones((v.shape[0], V_PAD - V_DIM), BF16)
        kr = kr_ref[s, 0]
        kr_tiles = (kr.astype(BF16), pltpu.roll(kr, QK_ROPE, axis=1).astype(BF16))
        for h in range(HEADS):
            k_ref[s, 0, h] = jnp.concatenate(
                [kn[:, h * QK_NOPE:(h + 1) * QK_NOPE].astype(BF16), kr_tiles[h % 2]], axis=1)
            v_ref[s, 0, h] = jnp.concatenate([v[:, h * V_DIM:(h + 1) * V_DIM].astype(BF16), ones], axis=1)


def _cache_expand(cache_ckv, cache_kr_pad, w_uk, w_uv):
    b, depth, p, _ = cache_ckv.shape
    group = 4
    return pl.pallas_call(
        _cache_kernel,
        grid=(depth, b // group),
        in_specs=[
            pl.BlockSpec((group, 1, p, KV_RANK), lambda l, i: (i, l, 0, 0)),
            pl.BlockSpec((group, 1, p, 128), lambda l, i: (i, l, 0, 0)),
            pl.BlockSpec((None, KV_RANK, HEADS * QK_NOPE), lambda l, i: (l, 0, 0)),
            pl.BlockSpec((None, KV_RANK, HEADS * V_DIM), lambda l, i: (l, 0, 0)),
        ],
        out_specs=[
            pl.BlockSpec((group, 1, HEADS, p, QK_PAD), lambda l, i: (i, l, 0, 0, 0)),
            pl.BlockSpec((group, 1, HEADS, p, V_PAD), lambda l, i: (i, l, 0, 0, 0)),
        ],
        out_shape=[
            jax.ShapeDtypeStruct((b, depth, HEADS, p, QK_PAD), BF16),
            jax.ShapeDtypeStruct((b, depth, HEADS, p, V_PAD), BF16),
        ],
        compiler_params=_params(32),
        name="cache_expand",
    )(cache_ckv, cache_kr_pad, w_uk, w_uv)


def _rows_up(a, k):
    return pltpu.roll(a, a.shape[0] - k, axis=0)


def _rows_down(a, k):
    return pltpu.roll(a, k, axis=0)


def _rope(x, c, s1, s2):
    return x * c + pltpu.roll(x, 96, axis=1) * s1 + pltpu.roll(x, 32, axis=1) * s2


def _pool_mix(hp_e, hp, t0, seq, r):
    tm = hp.shape[0]
    main = slice(HALO, HALO + tm)
    lane_grp = jax.lax.broadcasted_iota(jnp.int32, (1, 2 * POOL_GD), 1) // POOL_GD
    xa = hp_e[:, 0:2 * POOL_GD]
    s2 = xa + _rows_down(xa, 1)
    s4 = _rows_up(s2, 1) + _rows_down(s2, 1)
    win_a = jnp.where(lane_grp == 0, s2[main], s4[main])
    xb = hp_e[:, 2 * POOL_GD:POOL_W]
    f2 = xb + _rows_up(xb, 1)
    f4 = f2 + _rows_up(f2, 2)
    f8 = f4 + _rows_up(f4, 4)
    s8 = _rows_down(f8, 4)
    s16 = f8[0:tm] + f8[main]
    win_b = jnp.where(lane_grp == 0, s8[main], s16)
    win = jnp.concatenate([win_a, win_b], axis=1)
    grp = jax.lax.broadcasted_iota(jnp.int32, (1, POOL_W), 1) // POOL_GD
    half = jnp.where(grp == 0, 1, jnp.where(grp == 1, 2, jnp.where(grp == 2, 4, 8)))
    t = t0 + jax.lax.broadcasted_iota(jnp.int32, (tm, 1), 0)
    cnt = jnp.clip(t + half, 0, seq) - jnp.clip(t - half, 0, seq)
    pooled = win / cnt.astype(F32) - hp
    return _dot(pooled.astype(BF16), r["w_pool"][...]) * r["pool_scale"][...]


def _gmlp(proj_g, r):
    tm = proj_g.shape[0]
    uv = jax.nn.gelu(proj_g)
    u = uv[:, 0:GMLP_W]
    vg = _rms(uv[:, GMLP_W:], r["g_sgu"][...]).astype(BF16)
    ggrp = jax.lax.broadcasted_iota(jnp.int32, (1, GMLP_W), 1) // GMLP_GD
    ys = []
    for cix in range(tm // CHUNK):
        m = _dot(r["w_s"][...], vg[cix * CHUNK:(cix + 1) * CHUNK])
        mixed = jnp.where(
            ggrp == 0, m[0:CHUNK],
            jnp.where(ggrp == 1, m[CHUNK:2 * CHUNK],
                      jnp.where(ggrp == 2, m[2 * CHUNK:3 * CHUNK], m[3 * CHUNK:4 * CHUNK])))
        ys.append(u[cix * CHUNK:(cix + 1) * CHUNK] * (mixed + r["b_s"][...]))
    return jnp.concatenate(ys, axis=0)


def _mla_operands(proj, r, rope):
    cq = _rms(proj[:, P_Q:P_KV], r["g_q"][...])
    q = _dot(cq.astype(BF16), r["w_uq"][...])
    ckv = _rms(proj[:, P_KV:P_R], r["g_kv"][...])
    ckv_b = ckv.astype(BF16)
    kn = _dot(ckv_b, r["w_uk"][...])
    v = _dot(ckv_b, r["w_uv"][...])
    ones = jnp.ones((v.shape[0], V_PAD - V_DIM), BF16)
    kr_raw = proj[:, P_R:P_G]
    kr = kr_raw if rope is None else _rope(kr_raw, *rope)
    kr_tiles = (kr.astype(BF16), pltpu.roll(kr, QK_ROPE, axis=1).astype(BF16))
    qs, ks, vs = [], [], []
    for pair in range(HEADS // 2):
        base = pair * Q_PAIR
        qr = q[:, base + 2 * QK_NOPE:base + Q_PAIR]
        if rope is not None:
            qr = _rope(qr, *rope)
        qr_b = qr.astype(BF16)
        for j in range(2):
            hd = 2 * pair + j
            qn = q[:, base + j * QK_NOPE:base + (j + 1) * QK_NOPE]
            qs.append(jnp.concatenate([qn.astype(BF16), qr_b], axis=1))
            ks.append(jnp.concatenate([kn[:, hd * QK_NOPE:(hd + 1) * QK_NOPE].astype(BF16), kr_tiles[j]], axis=1))
            vs.append(jnp.concatenate([v[:, hd * V_DIM:(hd + 1) * V_DIM].astype(BF16), ones], axis=1))
    return qs, ks, vs, ckv, kr_raw


def _softmax_pv(s, sc, v, vc):
    m = jnp.max(s, axis=1, keepdims=True)
    if sc is not None:
        m = jnp.maximum(m, jnp.max(sc, axis=1, keepdims=True))
    oa = _dot(jnp.exp2(s - m).astype(BF16), v)
    if sc is not None:
        oa = oa + _dot(jnp.exp2(sc - m).astype(BF16), vc)
    return oa[:, 0:V_DIM] / oa[:, V_DIM:V_DIM + 1]


def _ffn_block(x, ycat, mod_ref, r, g_final):
    g1 = mod_ref[:, 2 * D:3 * D]
    sh2 = mod_ref[:, 3 * D:4 * D]
    sc2 = mod_ref[:, 4 * D:5 * D]
    g2 = mod_ref[:, 5 * D:6 * D]
    x1 = x + g1 * _dot(ycat, r["w_out"][...])
    h = _ada_rms(x1, r["g_ffn"][...], sc2, sh2).astype(BF16)
    acc = None
    for j in range(D_FF // FF_CHUNK):
        a = _dot(h, r["w_ff1"][:, j * FF_CHUNK:(j + 1) * FF_CHUNK])
        a = jnp.square(jnp.maximum(a, 0.0)).astype(BF16)
        part = _dot(a, r["w_ff2"][j * FF_CHUNK:(j + 1) * FF_CHUNK, :])
        acc = part if acc is None else acc + part
    out = x1 + g2 * acc
    if g_final is not None:
        out = _rms(out, g_final)
    return out


def _ctx_kernel(*refs, names, final):
    r = dict(zip(names, refs))
    mod_ref = r["mod"]
    group, seq, _ = r["x"].shape
    x = r["x"][...].reshape(group * seq, D)
    h = _ada_rms(x, r["g_mix"][...], mod_ref[:, D:2 * D], mod_ref[:, 0:D]).astype(BF16)
    proj = _dot(h, r["w_in"][...])
    proj_g = _dot(h, r["w_in_g"][...])
    y_g = _gmlp(proj_g, r)
    qs, ks, vs, ckv, kr_raw = _mla_operands(proj, r, None)
    r["sckv"][...] = ckv.reshape(group, seq, KV_RANK)
    r["skr"][...] = kr_raw[:, 0:QK_ROPE].reshape(group, seq, QK_ROPE)
    edge = jnp.zeros((HALO, POOL_W), F32)
    ycats = []
    for s in range(group):
        own = slice(s * seq, (s + 1) * seq)
        hp = proj[own, 0:POOL_W]
        y_pool = _pool_mix(jnp.concatenate([edge, hp, edge], axis=0), hp, 0, seq, r)
        heads = [_softmax_pv(_dot_nt(qs[hd][own], ks[hd][own]), None, vs[hd][own], None)
                 for hd in range(HEADS)]
        ycats.append(jnp.concatenate([y_pool] + heads + [y_g[own]], axis=1).astype(BF16))
    out = _ffn_block(x, jnp.concatenate(ycats, axis=0), mod_ref, r, r["g_final"][...] if final else None)
    r["o"][...] = out.reshape(group, seq, D)


def _ctx_layer(x, states, mod, layer, w, ffn, g_final):
    b, seq, _ = x.shape
    final = g_final is not None
    names = ["ckv_buf", "kr_buf", "x", "mod"] + list(MIXER_PARAMS) + list(FFN_PARAMS)
    in_specs = ([pl.BlockSpec(memory_space=pl.ANY), pl.BlockSpec(memory_space=pl.ANY),
                 pl.BlockSpec((CTX_GROUP, seq, D), lambda bi, i: (bi, 0, 0)), _mod_spec(0, 0)]
                + _param_specs(MIXER_PARAMS, MIXER_SHAPES, layer)
                + _param_specs(FFN_PARAMS, FFN_SHAPES, None))
    args = list(states) + [x, mod] + [w[n] for n in MIXER_PARAMS] + [ffn[n] for n in FFN_PARAMS]
    if final:
        names.append("g_final")
        in_specs.append(pl.BlockSpec((1, D), lambda bi, i: (0, 0)))
        args.append(g_final)
    names += ["o", "sckv", "skr"]
    return pl.pallas_call(
        functools.partial(_ctx_kernel, names=tuple(names), final=final),
        grid=(b // CTX_GROUP, 1),
        in_specs=in_specs,
        out_specs=[
            pl.BlockSpec((CTX_GROUP, seq, D), lambda bi, i: (bi, 0, 0)),
            pl.BlockSpec((CTX_GROUP, None, seq, KV_RANK), lambda bi, i: (bi, layer, 0, 0)),
            pl.BlockSpec((CTX_GROUP, None, seq, QK_ROPE), lambda bi, i: (bi, layer, 0, 0)),
        ],
        out_shape=[
            jax.ShapeDtypeStruct((b, seq, D), F32),
            jax.ShapeDtypeStruct(states[0].shape, F32),
            jax.ShapeDtypeStruct(states[1].shape, F32),
        ],
        input_output_aliases={0: 1, 1: 2},
        compiler_params=_params(52),
        name="ctx_layer",
    )(*args)


def _pre_kernel(*refs, names, tm, seq, cast):
    r = dict(zip(names, refs))
    mod_ref = r["mod"]
    i = pl.program_id(1)
    rows = tm + 2 * HALO
    xe = jnp.concatenate([r["x_prev"][0], r["x"][0], r["x_next"][0]], axis=0)
    h = _ada_rms(xe, r["g_mix"][...], mod_ref[:, D:2 * D], mod_ref[:, 0:D]).astype(BF16)
    proj_e = _dot(h, r["w_in"][...])
    proj = proj_e[HALO:HALO + tm]
    proj_g = _dot(h, r["w_in_g"][...])[HALO:HALO + tm]

    pos = i * tm - HALO + jax.lax.broadcasted_iota(jnp.int32, (rows, 1), 0)
    hp_e = jnp.where((pos >= 0) & (pos < seq), proj_e[:, 0:POOL_W], 0.0)
    y_pool = _pool_mix(hp_e, proj[:, 0:POOL_W], i * tm, seq, r)

    rope = (r["rope_c"][...], r["rope_s1"][...], r["rope_s2"][...])
    qs, ks, vs, _, _ = _mla_operands(proj, r, rope)
    for hd in range(HEADS):
        r["q"][0, hd] = qs[hd]
        r["k"][0, hd] = ks[hd]
        r["v"][0, hd] = vs[hd]

    y_g = _gmlp(proj_g, r)
    r["ypg"][0] = jnp.concatenate([y_pool, y_g], axis=1).astype(BF16)
    if cast:
        _cast_slabs_body(r)


def _pre(x, mod, layer, w, rope_tabs, cast_src, *, tm):
    b, seq, _ = x.shape
    nt = seq // tm
    hb = tm // HALO
    nhb = seq // HALO
    names = ["x", "x_prev", "x_next", "mod"] + list(MIXER_PARAMS) + ["rope_c", "rope_s1", "rope_s2"]
    out_names = ["q", "k", "v", "ypg"]
    in_specs = ([
        pl.BlockSpec((1, tm, D), lambda bi, i: (bi, i, 0)),
        pl.BlockSpec((1, HALO, D), lambda bi, i: (bi, jnp.maximum(i * hb - 1, 0), 0)),
        pl.BlockSpec((1, HALO, D), lambda bi, i: (bi, jnp.minimum((i + 1) * hb, nhb - 1), 0)),
        _mod_spec(1, 1),
    ] + _param_specs(MIXER_PARAMS, MIXER_SHAPES, layer)
      + [pl.BlockSpec((tm, 128), lambda bi, i: (i, 0))] * 3)
    args = [x, x, x, mod] + [w[n] for n in MIXER_PARAMS] + list(rope_tabs)
    out_specs = [
        pl.BlockSpec((1, HEADS, tm, QK_PAD), lambda bi, i: (bi, 0, i, 0)),
        pl.BlockSpec((1, HEADS, tm, QK_PAD), lambda bi, i: (bi, 0, i, 0)),
        pl.BlockSpec((1, HEADS, tm, V_PAD), lambda bi, i: (bi, 0, i, 0)),
        pl.BlockSpec((1, tm, 2 * GMLP_W), lambda bi, i: (bi, i, 0)),
    ]
    out_shape = [
        jax.ShapeDtypeStruct((b, HEADS, seq, QK_PAD), BF16),
        jax.ShapeDtypeStruct((b, HEADS, seq, QK_PAD), BF16),
        jax.ShapeDtypeStruct((b, HEADS, seq, V_PAD), BF16),
        jax.ShapeDtypeStruct((b, seq, 2 * GMLP_W), BF16),
    ]
    if cast_src is not None:
        side = _cast_slabs(cast_src, layer, nt, b * nt)
        names, in_specs, args = names + side["names"], in_specs + side["in_specs"], args + side["args"]
        out_names, out_specs = out_names + side["out_names"], out_specs + side["out_specs"]
        out_shape = out_shape + side["out_shape"]
    return pl.pallas_call(
        functools.partial(_pre_kernel, names=tuple(names + out_names), tm=tm, seq=seq,
                          cast=cast_src is not None),
        grid=(b, nt),
        in_specs=in_specs,
        out_specs=out_specs,
        out_shape=out_shape,
        compiler_params=_params(56),
        name="pre_lat",
    )(*args)


def _attn_kernel(*refs, names, tq, prep_next):
    r = dict(zip(names, refs))
    q_ref, k_ref, v_ref, kc_ref, vc_ref, o_ref = (r[n] for n in ("q", "k", "v", "kc", "vc", "o"))
    units = [(h, slice(u * Q_UNIT, (u + 1) * Q_UNIT)) for h in range(HEADS) for u in range(tq // Q_UNIT)]

    def scores(unit):
        h, rows = unit
        q = q_ref[0, h, rows, :]
        return _dot_nt(q, k_ref[0, h]), _dot_nt(q, kc_ref[0, 0, h])

    ahead = [scores(u) for u in units[:LOOKAHEAD]]
    for idx, (h, rows) in enumerate(units):
        s, sc = ahead.pop(0)
        if idx + LOOKAHEAD < len(units):
            ahead.append(scores(units[idx + LOOKAHEAD]))
        o = _softmax_pv(s, sc, v_ref[0, h], vc_ref[0, 0, h])
        o_ref[0, rows, h * V_DIM:(h + 1) * V_DIM] = o.astype(BF16)

    if prep_next:
        r["mod_next"][...] = _modulation(r["c_all"][...], r["w_ada"][...], r["b_ada"][...])
        _cast_slabs_body(r)


def _attn(q, k, v, cache, layer, nxt, *, tq):
    b, _, seq, _ = q.shape
    nt = seq // tq
    steps = b * nt
    kc, vc = cache
    p = kc.shape[3]
    names = ["q", "k", "v", "kc", "vc"]
    in_specs = [
        pl.BlockSpec((1, HEADS, tq, QK_PAD), lambda bi, i: (bi, 0, i, 0)),
        pl.BlockSpec((1, HEADS, seq, QK_PAD), lambda bi, i: (bi, 0, 0, 0)),
        pl.BlockSpec((1, HEADS, seq, V_PAD), lambda bi, i: (bi, 0, 0, 0)),
        pl.BlockSpec((1, 1, HEADS, p, QK_PAD), lambda bi, i: (bi, layer, 0, 0, 0)),
        pl.BlockSpec((1, 1, HEADS, p, V_PAD), lambda bi, i: (bi, layer, 0, 0, 0)),
    ]
    args = [q, k, v, kc, vc]
    out_names = ["o"]
    out_specs = [pl.BlockSpec((1, tq, HEADS * V_DIM), lambda bi, i: (bi, i, 0))]
    out_shape = [jax.ShapeDtypeStruct((b, seq, HEADS * V_DIM), BF16)]
    if nxt is not None:
        rows = nxt["c_all"].shape[0]
        tn = 6 * D // steps
        names += ["c_all", "w_ada", "b_ada"]
        in_specs += [
            pl.BlockSpec((rows, D), lambda bi, i: (0, 0)),
            pl.BlockSpec((None, D, tn), lambda bi, i: (layer + 1, 0, bi * nt + i)),
            pl.BlockSpec((None, 1, tn), lambda bi, i: (layer + 1, 0, bi * nt + i)),
        ]
        args += [nxt["c_all"], nxt["w_ada"], nxt["b_ada"]]
        out_names.append("mod_next")
        out_specs.append(pl.BlockSpec((rows, tn), lambda bi, i: (0, bi * nt + i)))
        out_shape.append(jax.ShapeDtypeStruct((rows, 6 * D), F32))
        side = _cast_slabs(nxt, layer + 1, nt, steps)
        names, in_specs, args = names + side["names"], in_specs + side["in_specs"], args + side["args"]
        out_names, out_specs = out_names + side["out_names"], out_specs + side["out_specs"]
        out_shape = out_shape + side["out_shape"]
    return pl.pallas_call(
        functools.partial(_attn_kernel, names=tuple(names + out_names), tq=tq, prep_next=nxt is not None),
        grid=(b, nt),
        in_specs=in_specs,
        out_specs=out_specs,
        out_shape=out_shape,
        compiler_params=_params(52),
        name="attn_lat",
    )(*args)


def _post_kernel(*refs, names, final):
    r = dict(zip(names, refs))
    ypg = r["ypg"][0]
    ycat = jnp.concatenate([ypg[:, 0:POOL_W], r["ymla"][0], ypg[:, POOL_W:]], axis=1)
    r["o"][0] = _ffn_block(r["x"][0], ycat, r["mod"], r, r["g_final"][...] if final else None)


def _post(x, ypg, ymla, mod, ffn, g_final, *, tm):
    b, seq, _ = x.shape
    final = g_final is not None
    names = ["x", "ypg", "ymla", "mod"] + list(FFN_PARAMS)
    in_specs = [
        pl.BlockSpec((1, tm, D), lambda bi, i: (bi, i, 0)),
        pl.BlockSpec((1, tm, 2 * GMLP_W), lambda bi, i: (bi, i, 0)),
        pl.BlockSpec((1, tm, HEADS * V_DIM), lambda bi, i: (bi, i, 0)),
        _mod_spec(1, 1),
    ] + _param_specs(FFN_PARAMS, FFN_SHAPES, None)
    args = [x, ypg, ymla, mod] + [ffn[n] for n in FFN_PARAMS]
    if final:
        names.append("g_final")
        in_specs.append(pl.BlockSpec((1, D), lambda bi, i: (0, 0)))
        args.append(g_final)
    names.append("o")
    return pl.pallas_call(
        functools.partial(_post_kernel, names=tuple(names), final=final),
        grid=(b, seq // tm),
        in_specs=in_specs,
        out_specs=pl.BlockSpec((1, tm, D), lambda bi, i: (bi, i, 0)),
        out_shape=jax.ShapeDtypeStruct((b, seq, D), F32),
        compiler_params=_params(56),
        name="post_lat",
    )(*args)


def _rope_tables(seq):
    rows = seq // GRID_W
    row = jnp.repeat(jnp.arange(rows, dtype=F32), GRID_W)
    col = jnp.tile(jnp.arange(GRID_W, dtype=F32), rows)
    n_freq = QK_ROPE // 4
    inv = 1.0 / (ROPE_THETA ** (jnp.arange(n_freq, dtype=F32) / n_freq))
    ang = jnp.concatenate([row[:, None] * inv, col[:, None] * inv], axis=-1)
    cos, sin = jnp.cos(ang), jnp.sin(ang)
    z = jnp.zeros_like(cos)
    c = jnp.concatenate([cos, cos, cos, cos], axis=-1)
    s1 = jnp.concatenate([-sin, z, -sin, z], axis=-1)
    s2 = jnp.concatenate([z, sin, z, sin], axis=-1)
    return c, s1, s2


def kernel(x_prompt, x_sample, cache_ckv, cache_krope, c, c_ctx, w_ada, b_ada, g_mix, w_in, w_pool,
           pool_scale, g_q, w_uq, g_kv, w_ukv, g_sgu, w_s, b_s, w_out, g_ffn, w_ff1, w_ff2, g_final):
    dec_b = x_sample.shape[0]
    w_in_p, w_in_g = _win_prep(w_in)
    w_uq_h = w_uq.astype(BF16).reshape(DEPTH, Q_RANK, HEADS // 2, 2, QK_NOPE + QK_ROPE)
    w_uq_p = jnp.concatenate([w_uq_h[..., 0, :QK_NOPE], w_uq_h[..., 1, :QK_NOPE],
                              w_uq_h[..., 0, QK_NOPE:], w_uq_h[..., 1, QK_NOPE:]], axis=-1)
    w_uq_p = w_uq_p.reshape(DEPTH, Q_RANK, HEADS // 2 * Q_PAIR)
    w_ukv_h = w_ukv.astype(BF16).reshape(DEPTH, KV_RANK, HEADS, QK_NOPE + V_DIM)
    w_uk = w_ukv_h[..., :QK_NOPE].reshape(DEPTH, KV_RANK, HEADS * QK_NOPE)
    w_uv = w_ukv_h[..., QK_NOPE:].reshape(DEPTH, KV_RANK, HEADS * V_DIM)
    w_pool_bd = jnp.zeros((DEPTH, POOL_W, POOL_W), BF16)
    for g in range(len(POOL_WINDOWS)):
        sl = slice(g * POOL_GD, (g + 1) * POOL_GD)
        w_pool_bd = w_pool_bd.at[:, sl, sl].set(w_pool[:, g].astype(BF16))
    weights = {
        "g_mix": g_mix.reshape(DEPTH, 1, D),
        "w_in": w_in_p,
        "w_in_g": w_in_g,
        "g_q": (g_q * SM_SCALE).reshape(DEPTH, 1, Q_RANK),
        "w_uq": w_uq_p,
        "g_kv": g_kv.reshape(DEPTH, 1, KV_RANK),
        "w_uk": w_uk,
        "w_uv": w_uv,
        "w_pool": w_pool_bd,
        "pool_scale": pool_scale.reshape(DEPTH, 1, POOL_W),
        "g_sgu": g_sgu.reshape(DEPTH, 1, GMLP_W),
        "w_s": w_s.reshape(DEPTH, GMLP_G * CHUNK, CHUNK).astype(BF16),
        "b_s": jnp.repeat(jnp.swapaxes(b_s, 1, 2), GMLP_GD, axis=2),
    }
    g_fin = g_final.reshape(1, D)
    rope_tabs = _rope_tables(x_sample.shape[1])

    n_rows = 16
    c_all = jnp.zeros((n_rows, D), F32).at[0].set(c_ctx).at[1:1 + dec_b].set(c)
    b_ada3 = b_ada.reshape(DEPTH, 1, 6 * D)

    cache_kr_pad = jnp.pad(cache_krope, ((0, 0), (0, 0), (0, 0), (0, 128 - QK_ROPE)))
    cache = _cache_expand(cache_ckv, cache_kr_pad, w_uk, w_uv)

    mod = _ada_first(c_all, w_ada, b_ada3).reshape(n_rows, 1, 6 * D)
    f32_params = {"c_all": c_all, "w_ada": w_ada, "b_ada": b_ada3, "w_out": w_out, "w_ff1": w_ff1, "w_ff2": w_ff2}

    xp, xs = x_prompt, x_sample
    ctx_b, ctx_seq, _ = x_prompt.shape
    states = (jnp.zeros((ctx_b, DEPTH, ctx_seq, KV_RANK), F32), jnp.zeros((ctx_b, DEPTH, ctx_seq, QK_ROPE), F32))
    for l in range(DEPTH):
        last = l == DEPTH - 1
        g_last = g_fin if last else None
        q, k, v, ypg, *cast = _pre(xs, mod, l, weights, rope_tabs, f32_params if l == 0 else None, tm=1024)
        if l == 0:
            ffn = dict(zip(CAST_PARAMS, cast))
        ffn["g_ffn"] = g_ffn[l].reshape(1, D)
        xp, *states = _ctx_layer(xp, states, mod, l, weights, ffn, g_last)
        outs = _attn(q, k, v, cache, l, None if last else f32_params, tq=1024)
        xs = _post(xs, ypg, outs[0], mod, ffn, g_last, tm=1024)
        if not last:
            mod = outs[1].reshape(n_rows, 1, 6 * D)
            ffn = dict(zip(CAST_PARAMS, outs[2:]))
    return xp, xs, states[0], states[1]
```

```python
import functools
import math

import jax
import jax.numpy as jnp
from jax.experimental import pallas as pl
from jax.experimental.pallas import tpu as pltpu

D = 1024
DEPTH = 4
GRID_W = 64
EPS = 1e-6
POOL_W = 256
POOL_WINDOWS = (2, 4, 8, 16)
POOL_GD = 64
HALO = 8
QK_NOPE = 128
QK_ROPE = 64
V_DIM = 128
V_PAD = 2 * V_DIM
HEADS = 4
Q_RANK = 384
KV_RANK = 256
ROPE_THETA = 10000.0
CHUNK = 128
GMLP_W = 256
GMLP_G = 4
GMLP_GD = 64
D_FF = 4096
FF_CHUNK = 1024
P_Q = 256
P_KV = P_Q + Q_RANK
P_R = P_KV + KV_RANK
P_G = P_R + 128
QK_PAD = 256
Q_PAIR = 2 * QK_NOPE + 2 * QK_ROPE
Q_UNIT = 512
CTX_GROUP = 2
PRE_SPLIT = 2
LOOKAHEAD = 1
SM_SCALE = math.log2(math.e) / math.sqrt(QK_NOPE + QK_ROPE)
NT_DIMS = (((1,), (1,)), ((), ()))

F32 = jnp.float32
BF16 = jnp.bfloat16
MIB = 1024 * 1024

MIXER_PARAMS = ("g_mix", "w_in", "w_in_g", "g_q", "w_uq", "g_kv", "w_uk", "w_uv",
                "w_pool", "pool_scale", "g_sgu", "w_s", "b_s")
MIXER_SHAPES = {
    "g_mix": (1, D), "w_in": (D, P_G), "w_in_g": (D, 2 * GMLP_W), "g_q": (1, Q_RANK),
    "w_uq": (Q_RANK, HEADS // 2 * Q_PAIR), "g_kv": (1, KV_RANK), "w_uk": (KV_RANK, HEADS * QK_NOPE),
    "w_uv": (KV_RANK, HEADS * V_DIM), "w_pool": (POOL_W, POOL_W), "pool_scale": (1, POOL_W),
    "g_sgu": (1, GMLP_W), "w_s": (GMLP_G * CHUNK, CHUNK), "b_s": (CHUNK, GMLP_W),
}
FFN_PARAMS = ("w_out", "g_ffn", "w_ff1", "w_ff2")
FFN_SHAPES = {"w_out": (D, D), "g_ffn": (1, D), "w_ff1": (D, D_FF), "w_ff2": (D_FF, D)}
SINGLE_BUFFERED = ("w_in", "w_out", "w_ff1", "w_ff2")


def _params(vmem_mib):
    return pltpu.CompilerParams(
        dimension_semantics=("arbitrary", "arbitrary"),
        vmem_limit_bytes=vmem_mib * MIB,
    )


def _rms(x, g):
    y = x * jax.lax.rsqrt(jnp.mean(x * x, axis=-1, keepdims=True) + EPS)
    return y * g


def _ada_rms(x, g, scale, shift):
    y = x * jax.lax.rsqrt(jnp.mean(x * x, axis=-1, keepdims=True) + EPS)
    return y * (g * (1.0 + scale)) + shift


def _dot(a, b):
    return jnp.dot(a, b, preferred_element_type=F32)


def _dot_nt(a, b):
    return jax.lax.dot_general(a, b, NT_DIMS, preferred_element_type=F32)


def _slab_spec(shape, layer, single=False):
    if layer is None:
        block, index_map = shape, (lambda bi, i: (0,) * len(shape))
    else:
        block, index_map = (None,) + shape, (lambda bi, i: (layer,) + (0,) * len(shape))
    if single:
        return pl.BlockSpec(block, index_map, pipeline_mode=pl.Buffered(1))
    return pl.BlockSpec(block, index_map)


def _param_specs(names, shapes, layer):
    return [_slab_spec(shapes[n], layer, n in SINGLE_BUFFERED) for n in names]


def _mod_spec(row0, per_batch):
    return pl.BlockSpec((None, 1, 6 * D), lambda bi, i: (bi * per_batch + row0, 0, 0))


CAST_PARAMS = ("w_out", "w_ff1", "w_ff2")


def _cast_slabs(src, layer, nt, steps):
    side = {"names": [], "in_specs": [], "args": [], "out_names": [], "out_specs": [], "out_shape": []}
    for n in CAST_PARAMS:
        k_dim, n_dim = FFN_SHAPES[n]
        tr = k_dim // steps
        side["names"].append(n + "_f32")
        side["in_specs"].append(pl.BlockSpec((None, tr, n_dim), lambda bi, i: (layer, bi * nt + i, 0)))
        side["args"].append(src[n])
        side["out_names"].append(n + "_bf16")
        side["out_specs"].append(pl.BlockSpec((tr, n_dim), lambda bi, i: (bi * nt + i, 0)))
        side["out_shape"].append(jax.ShapeDtypeStruct((k_dim, n_dim), BF16))
    return side


def _cast_slabs_body(r):
    for n in CAST_PARAMS:
        r[n + "_bf16"][...] = r[n + "_f32"][...].astype(BF16)


def _modulation(c, w, b):
    s = c / (1.0 + jnp.exp(-c))
    return _dot(s.astype(BF16), w.astype(BF16)) + b


def _ada_kernel(c_ref, w_ref, b_ref, o_ref):
    o_ref[...] = _modulation(c_ref[...], w_ref[...], b_ref[...])


def _ada_first(c_all, w_ada, b_ada):
    rows = c_all.shape[0]
    tn = 1536
    return pl.pallas_call(
        _ada_kernel,
        grid=(1, 6 * D // tn),
        in_specs=[
            pl.BlockSpec((rows, D), lambda l, j: (0, 0)),
            pl.BlockSpec((None, D, tn), lambda l, j: (0, 0, j)),
            pl.BlockSpec((None, 1, tn), lambda l, j: (0, 0, j)),
        ],
        out_specs=pl.BlockSpec((rows, tn), lambda l, j: (0, j)),
        out_shape=jax.ShapeDtypeStruct((rows, 6 * D), F32),
        compiler_params=_params(32),
        name="ada",
    )(c_all, w_ada, b_ada)


def _win_prep_kernel(w_ref, wa_ref, wg_ref):
    lane = jax.lax.broadcasted_iota(jnp.int32, (1, P_G - P_R), 1)
    wa_ref[:, 0:P_R] = w_ref[:, 0:P_R].astype(BF16)
    wa_ref[:, P_R:P_G] = jnp.where(lane < QK_ROPE, w_ref[:, P_R:P_G], 0.0).astype(BF16)
    wg_ref[...] = w_ref[:, P_R + QK_ROPE:].astype(BF16)


def _win_prep(w_in):
    cols = w_in.shape[2]
    return pl.pallas_call(
        _win_prep_kernel,
        grid=(DEPTH,),
        in_specs=[pl.BlockSpec((None, D, cols), lambda l: (l, 0, 0))],
        out_specs=[
            pl.BlockSpec((None, D, P_G), lambda l: (l, 0, 0)),
            pl.BlockSpec((None, D, 2 * GMLP_W), lambda l: (l, 0, 0)),
        ],
        out_shape=[
            jax.ShapeDtypeStruct((DEPTH, D, P_G), BF16),
            jax.ShapeDtypeStruct((DEPTH, D, 2 * GMLP_W), BF16),
        ],
        compiler_params=pltpu.CompilerParams(dimension_semantics=("arbitrary",), vmem_limit_bytes=32 * MIB),
        name="win_prep",
    )(w_in)


def _cache_kernel(ckv_ref, kr_ref, wk_ref, wv_ref, k_ref, v_ref):
    for s in range(ckv_ref.shape[0]):
        ckv = ckv_ref[s, 0].astype(BF16)
        kn = _dot(ckv, wk_ref[...])
        v = _dot(ckv, wv_ref[...])
        ones = jnp.ones((v.shape[0], V_PAD - V_DIM), BF16)
        kr = kr_ref[s, 0]
        kr_tiles = (kr.astype(BF16), pltpu.roll(kr, QK_ROPE, axis=1).astype(BF16))
        for h in range(HEADS):
            k_ref[s, 0, h] = jnp.concatenate(
                [kn[:, h * QK_NOPE:(h + 1) * QK_NOPE].astype(BF16), kr_tiles[h % 2]], axis=1)
            v_ref[s, 0, h] = jnp.concatenate([v[:, h * V_DIM:(h + 1) * V_DIM].astype(BF16), ones], axis=1)


def _cache_expand(cache_ckv, cache_kr_pad, w_uk, w_uv):
    b, depth, p, _ = cache_ckv.shape
    group = 4
    return pl.pallas_call(
        _cache_kernel,
        grid=(depth, b // group),
        in_specs=[
            pl.BlockSpec((group, 1, p, KV_RANK), lambda l, i: (i, l, 0, 0)),
            pl.BlockSpec((group, 1, p, 128), lambda l, i: (i, l, 0, 0)),
            pl.BlockSpec((None, KV_RANK, HEADS * QK_NOPE), lambda l, i: (l, 0, 0)),
            pl.BlockSpec((None, KV_RANK, HEADS * V_DIM), lambda l, i: (l, 0, 0)),
        ],
        out_specs=[
            pl.BlockSpec((group, 1, HEADS, p, QK_PAD), lambda l, i: (i, l, 0, 0, 0)),
            pl.BlockSpec((group, 1, HEADS, p, V_PAD), lambda l, i: (i, l, 0, 0, 0)),
        ],
        out_shape=[
            jax.ShapeDtypeStruct((b, depth, HEADS, p, QK_PAD), BF16),
            jax.ShapeDtypeStruct((b, depth, HEADS, p, V_PAD), BF16),
        ],
        compiler_params=_params(32),
        name="cache_expand",
    )(cache_ckv, cache_kr_pad, w_uk, w_uv)


def _rows_up(a, k):
    return pltpu.roll(a, a.shape[0] - k, axis=0)


def _rows_down(a, k):
    return pltpu.roll(a, k, axis=0)


def _rope(x, c, s1, s2):
    return x * c + pltpu.roll(x, 96, axis=1) * s1 + pltpu.roll(x, 32, axis=1) * s2


def _pool_mix(hp_e, hp, t0, seq, r):
    tm = hp.shape[0]
    main = slice(HALO, HALO + tm)
    lane_grp = jax.lax.broadcasted_iota(jnp.int32, (1, 2 * POOL_GD), 1) // POOL_GD
    xa = hp_e[:, 0:2 * POOL_GD]
    s2 = xa + _rows_down(xa, 1)
    s4 = _rows_up(s2, 1) + _rows_down(s2, 1)
    win_a = jnp.where(lane_grp == 0, s2[main], s4[main])
    xb = hp_e[:, 2 * POOL_GD:POOL_W]
    f2 = xb + _rows_up(xb, 1)
    f4 = f2 + _rows_up(f2, 2)
    f8 = f4 + _rows_up(f4, 4)
    s8 = _rows_down(f8, 4)
    s16 = f8[0:tm] + f8[main]
    win_b = jnp.where(lane_grp == 0, s8[main], s16)
    win = jnp.concatenate([win_a, win_b], axis=1)
    grp = jax.lax.broadcasted_iota(jnp.int32, (1, POOL_W), 1) // POOL_GD
    half = jnp.where(grp == 0, 1, jnp.where(grp == 1, 2, jnp.where(grp == 2, 4, 8)))
    t = t0 + jax.lax.broadcasted_iota(jnp.int32, (tm, 1), 0)
    cnt = jnp.clip(t + half, 0, seq) - jnp.clip(t - half, 0, seq)
    pooled = win / cnt.astype(F32) - hp
    return _dot(pooled.astype(BF16), r["w_pool"][...]) * r["pool_scale"][...]


def _gmlp(proj_g, r):
    tm = proj_g.shape[0]
    uv = jax.nn.gelu(proj_g)
    u = uv[:, 0:GMLP_W]
    vg = _rms(uv[:, GMLP_W:], r["g_sgu"][...]).astype(BF16)
    ggrp = jax.lax.broadcasted_iota(jnp.int32, (1, GMLP_W), 1) // GMLP_GD
    ys = []
    for cix in range(tm // CHUNK):
        m = _dot(r["w_s"][...], vg[cix * CHUNK:(cix + 1) * CHUNK])
        mixed = jnp.where(
            ggrp == 0, m[0:CHUNK],
            jnp.where(ggrp == 1, m[CHUNK:2 * CHUNK],
                      jnp.where(ggrp == 2, m[2 * CHUNK:3 * CHUNK], m[3 * CHUNK:4 * CHUNK])))
        ys.append(u[cix * CHUNK:(cix + 1) * CHUNK] * (mixed + r["b_s"][...]))
    return jnp.concatenate(ys, axis=0)


def _mla_operands(proj, r, rope):
    cq = _rms(proj[:, P_Q:P_KV], r["g_q"][...])
    q = _dot(cq.astype(BF16), r["w_uq"][...])
    ckv = _rms(proj[:, P_KV:P_R], r["g_kv"][...])
    ckv_b = ckv.astype(BF16)
    kn = _dot(ckv_b, r["w_uk"][...])
    v = _dot(ckv_b, r["w_uv"][...])
    ones = jnp.ones((v.shape[0], V_PAD - V_DIM), BF16)
    kr_raw = proj[:, P_R:P_G]
    kr = kr_raw if rope is None else _rope(kr_raw, *rope)
    kr_tiles = (kr.astype(BF16), pltpu.roll(kr, QK_ROPE, axis=1).astype(BF16))
    qs, ks, vs = [], [], []
    for pair in range(HEADS // 2):
        base = pair * Q_PAIR
        qr = q[:, base + 2 * QK_NOPE:base + Q_PAIR]
        if rope is not None:
            qr = _rope(qr, *rope)
        qr_b = qr.astype(BF16)
        for j in range(2):
            hd = 2 * pair + j
            qn = q[:, base + j * QK_NOPE:base + (j + 1) * QK_NOPE]
            qs.append(jnp.concatenate([qn.astype(BF16), qr_b], axis=1))
            ks.append(jnp.concatenate([kn[:, hd * QK_NOPE:(hd + 1) * QK_NOPE].astype(BF16), kr_tiles[j]], axis=1))
            vs.append(jnp.concatenate([v[:, hd * V_DIM:(hd + 1) * V_DIM].astype(BF16), ones], axis=1))
    return qs, ks, vs, ckv, kr_raw


def _softmax_pv(s, sc, v, vc):
    m = jnp.max(s, axis=1, keepdims=True)
    if sc is not None:
        m = jnp.maximum(m, jnp.max(sc, axis=1, keepdims=True))
    oa = _dot(jnp.exp2(s - m).astype(BF16), v)
    if sc is not None:
        oa = oa + _dot(jnp.exp2(sc - m).astype(BF16), vc)
    return oa[:, 0:V_DIM] / oa[:, V_DIM:V_DIM + 1]


def _ffn_block(x, ycat, mod_ref, r, g_final):
    g1 = mod_ref[:, 2 * D:3 * D]
    sh2 = mod_ref[:, 3 * D:4 * D]
    sc2 = mod_ref[:, 4 * D:5 * D]
    g2 = mod_ref[:, 5 * D:6 * D]
    x1 = x + g1 * _dot(ycat, r["w_out"][...])
    h = _ada_rms(x1, r["g_ffn"][...], sc2, sh2).astype(BF16)
    acc = None
    for j in range(D_FF // FF_CHUNK):
        a = _dot(h, r["w_ff1"][:, j * FF_CHUNK:(j + 1) * FF_CHUNK])
        a = jnp.square(jnp.maximum(a, 0.0)).astype(BF16)
        part = _dot(a, r["w_ff2"][j * FF_CHUNK:(j + 1) * FF_CHUNK, :])
        acc = part if acc is None else acc + part
    out = x1 + g2 * acc
    if g_final is not None:
        out = _rms(out, g_final)
    return out


def _ctx_kernel(*refs, names, final):
    r = dict(zip(names, refs))
    mod_ref = r["mod"]
    group, seq, _ = r["x"].shape
    x = r["x"][...].reshape(group * seq, D)
    h = _ada_rms(x, r["g_mix"][...], mod_ref[:, D:2 * D], mod_ref[:, 0:D]).astype(BF16)
    proj = _dot(h, r["w_in"][...])
    proj_g = _dot(h, r["w_in_g"][...])
    y_g = _gmlp(proj_g, r)
    qs, ks, vs, ckv, kr_raw = _mla_operands(proj, r, None)
    r["sckv"][...] = ckv.reshape(group, seq, KV_RANK)
    r["skr"][...] = kr_raw[:, 0:QK_ROPE].reshape(group, seq, QK_ROPE)
    edge = jnp.zeros((HALO, POOL_W), F32)
    ycats = []
    for s in range(group):
        own = slice(s * seq, (s + 1) * seq)
        hp = proj[own, 0:POOL_W]
        y_pool = _pool_mix(jnp.concatenate([edge, hp, edge], axis=0), hp, 0, seq, r)
        heads = [_softmax_pv(_dot_nt(qs[hd][own], ks[hd][own]), None, vs[hd][own], None)
                 for hd in range(HEADS)]
        ycats.append(jnp.concatenate([y_pool] + heads + [y_g[own]], axis=1).astype(BF16))
    out = _ffn_block(x, jnp.concatenate(ycats, axis=0), mod_ref, r, r["g_final"][...] if final else None)
    r["o"][...] = out.reshape(group, seq, D)


def _ctx_layer(x, states, mod, layer, w, ffn, g_final):
    b, seq, _ = x.shape
    final = g_final is not None
    names = ["ckv_buf", "kr_buf", "x", "mod"] + list(MIXER_PARAMS) + list(FFN_PARAMS)
    in_specs = ([pl.BlockSpec(memory_space=pl.ANY), pl.BlockSpec(memory_space=pl.ANY),
                 pl.BlockSpec((CTX_GROUP, seq, D), lambda bi, i: (bi, 0, 0)), _mod_spec(0, 0)]
                + _param_specs(MIXER_PARAMS, MIXER_SHAPES, layer)
                + _param_specs(FFN_PARAMS, FFN_SHAPES, None))
    args = list(states) + [x, mod] + [w[n] for n in MIXER_PARAMS] + [ffn[n] for n in FFN_PARAMS]
    if final:
        names.append("g_final")
        in_specs.append(pl.BlockSpec((1, D), lambda bi, i: (0, 0)))
        args.append(g_final)
    names += ["o", "sckv", "skr"]
    return pl.pallas_call(
        functools.partial(_ctx_kernel, names=tuple(names), final=final),
        grid=(b // CTX_GROUP, 1),
        in_specs=in_specs,
        out_specs=[
            pl.BlockSpec((CTX_GROUP, seq, D), lambda bi, i: (bi, 0, 0)),
            pl.BlockSpec((CTX_GROUP, None, seq, KV_RANK), lambda bi, i: (bi, layer, 0, 0)),
            pl.BlockSpec((CTX_GROUP, None, seq, QK_ROPE), lambda bi, i: (bi, layer, 0, 0)),
        ],
        out_shape=[
            jax.ShapeDtypeStruct((b, seq, D), F32),
            jax.ShapeDtypeStruct(states[0].shape, F32),
            jax.ShapeDtypeStruct(states[1].shape, F32),
        ],
        input_output_aliases={0: 1, 1: 2},
        compiler_params=_params(52),
        name="ctx_layer",
    )(*args)


def _pre_kernel(*refs, names, tm, seq, cast):
    r = dict(zip(names, refs))
    mod_ref = r["mod"]
    i = pl.program_id(1)
    ts = tm // PRE_SPLIT
    rows = ts + 2 * HALO

    def project(j):
        lo, hi = j * ts, (j + 1) * ts
        before = r["x_prev"][0] if j == 0 else r["x"][0, lo - HALO:lo, :]
        after = r["x_next"][0] if j == PRE_SPLIT - 1 else r["x"][0, hi:hi + HALO, :]
        xe = jnp.concatenate([before, r["x"][0, lo:hi, :], after], axis=0)
        h = _ada_rms(xe, r["g_mix"][...], mod_ref[:, D:2 * D], mod_ref[:, 0:D]).astype(BF16)
        proj_e = _dot(h, r["w_in"][...])
        proj_g = _dot(h, r["w_in_g"][...])[HALO:HALO + ts]
        return proj_e, proj_g

    ahead = project(0)
    for j in range(PRE_SPLIT):
        proj_e, proj_g = ahead
        if j + 1 < PRE_SPLIT:
            ahead = project(j + 1)
        sub = slice(j * ts, (j + 1) * ts)
        t0 = i * tm + j * ts
        proj = proj_e[HALO:HALO + ts]
        pos = t0 - HALO + jax.lax.broadcasted_iota(jnp.int32, (rows, 1), 0)
        hp_e = jnp.where((pos >= 0) & (pos < seq), proj_e[:, 0:POOL_W], 0.0)
        y_pool = _pool_mix(hp_e, proj[:, 0:POOL_W], t0, seq, r)

        rope = (r["rope_c"][sub, :], r["rope_s1"][sub, :], r["rope_s2"][sub, :])
        qs, ks, vs, _, _ = _mla_operands(proj, r, rope)
        for hd in range(HEADS):
            r["q"][0, hd, sub, :] = qs[hd]
            r["k"][0, hd, sub, :] = ks[hd]
            r["v"][0, hd, sub, :] = vs[hd]

        y_g = _gmlp(proj_g, r)
        r["ypg"][0, sub, :] = jnp.concatenate([y_pool, y_g], axis=1).astype(BF16)
    if cast:
        _cast_slabs_body(r)


def _pre(x, mod, layer, w, rope_tabs, cast_src, *, tm):
    b, seq, _ = x.shape
    nt = seq // tm
    hb = tm // HALO
    nhb = seq // HALO
    names = ["x", "x_prev", "x_next", "mod"] + list(MIXER_PARAMS) + ["rope_c", "rope_s1", "rope_s2"]
    out_names = ["q", "k", "v", "ypg"]
    in_specs = ([
        pl.BlockSpec((1, tm, D), lambda bi, i: (bi, i, 0)),
        pl.BlockSpec((1, HALO, D), lambda bi, i: (bi, jnp.maximum(i * hb - 1, 0), 0)),
        pl.BlockSpec((1, HALO, D), lambda bi, i: (bi, jnp.minimum((i + 1) * hb, nhb - 1), 0)),
        _mod_spec(1, 1),
    ] + _param_specs(MIXER_PARAMS, MIXER_SHAPES, layer)
      + [pl.BlockSpec((tm, 128), lambda bi, i: (i, 0))] * 3)
    args = [x, x, x, mod] + [w[n] for n in MIXER_PARAMS] + list(rope_tabs)
    out_specs = [
        pl.BlockSpec((1, HEADS, tm, QK_PAD), lambda bi, i: (bi, 0, i, 0)),
        pl.BlockSpec((1, HEADS, tm, QK_PAD), lambda bi, i: (bi, 0, i, 0)),
        pl.BlockSpec((1, HEADS, tm, V_PAD), lambda bi, i: (bi, 0, i, 0)),
        pl.BlockSpec((1, tm, 2 * GMLP_W), lambda bi, i: (bi, i, 0)),
    ]
    out_shape = [
        jax.ShapeDtypeStruct((b, HEADS, seq, QK_PAD), BF16),
        jax.ShapeDtypeStruct((b, HEADS, seq, QK_PAD), BF16),
        jax.ShapeDtypeStruct((b, HEADS, seq, V_PAD), BF16),
        jax.ShapeDtypeStruct((b, seq, 2 * GMLP_W), BF16),
    ]
    if cast_src is not None:
        side = _cast_slabs(cast_src, layer, nt, b * nt)
        names, in_specs, args = names + side["names"], in_specs + side["in_specs"], args + side["args"]
        out_names, out_specs = out_names + side["out_names"], out_specs + side["out_specs"]
        out_shape = out_shape + side["out_shape"]
    return pl.pallas_call(
        functools.partial(_pre_kernel, names=tuple(names + out_names), tm=tm, seq=seq,
                          cast=cast_src is not None),
        grid=(b, nt),
        in_specs=in_specs,
        out_specs=out_specs,
        out_shape=out_shape,
        compiler_params=_params(56),
        name="pre_lat",
    )(*args)


def _attn_kernel(*refs, names, tq, prep_next):
    r = dict(zip(names, refs))
    q_ref, k_ref, v_ref, kc_ref, vc_ref, o_ref = (r[n] for n in ("q", "k", "v", "kc", "vc", "o"))
    units = [(h, slice(u * Q_UNIT, (u + 1) * Q_UNIT)) for h in range(HEADS) for u in range(tq // Q_UNIT)]

    def scores(unit):
        h, rows = unit
        q = q_ref[0, h, rows, :]
        return _dot_nt(q, k_ref[0, h]), _dot_nt(q, kc_ref[0, 0, h])

    ahead = [scores(u) for u in units[:LOOKAHEAD]]
    for idx, (h, rows) in enumerate(units):
        s, sc = ahead.pop(0)
        if idx + LOOKAHEAD < len(units):
            ahead.append(scores(units[idx + LOOKAHEAD]))
        o = _softmax_pv(s, sc, v_ref[0, h], vc_ref[0, 0, h])
        o_ref[0, rows, h * V_DIM:(h + 1) * V_DIM] = o.astype(BF16)

    if prep_next:
        r["mod_next"][...] = _modulation(r["c_all"][...], r["w_ada"][...], r["b_ada"][...])
        _cast_slabs_body(r)


def _attn(q, k, v, cache, layer, nxt, *, tq):
    b, _, seq, _ = q.shape
    nt = seq // tq
    steps = b * nt
    kc, vc = cache
    p = kc.shape[3]
    names = ["q", "k", "v", "kc", "vc"]
    in_specs = [
        pl.BlockSpec((1, HEADS, tq, QK_PAD), lambda bi, i: (bi, 0, i, 0)),
        pl.BlockSpec((1, HEADS, seq, QK_PAD), lambda bi, i: (bi, 0, 0, 0)),
        pl.BlockSpec((1, HEADS, seq, V_PAD), lambda bi, i: (bi, 0, 0, 0)),
        pl.BlockSpec((1, 1, HEADS, p, QK_PAD), lambda bi, i: (bi, layer, 0, 0, 0)),
        pl.BlockSpec((1, 1, HEADS, p, V_PAD), lambda bi, i: (bi, layer, 0, 0, 0)),
    ]
    args = [q, k, v, kc, vc]
    out_names = ["o"]
    out_specs = [pl.BlockSpec((1, tq, HEADS * V_DIM), lambda bi, i: (bi, i, 0))]
    out_shape = [jax.ShapeDtypeStruct((b, seq, HEADS * V_DIM), BF16)]
    if nxt is not None:
        rows = nxt["c_all"].shape[0]
        tn = 6 * D // steps
        names += ["c_all", "w_ada", "b_ada"]
        in_specs += [
            pl.BlockSpec((rows, D), lambda bi, i: (0, 0)),
            pl.BlockSpec((None, D, tn), lambda bi, i: (layer + 1, 0, bi * nt + i)),
            pl.BlockSpec((None, 1, tn), lambda bi, i: (layer + 1, 0, bi * nt + i)),
        ]
        args += [nxt["c_all"], nxt["w_ada"], nxt["b_ada"]]
        out_names.append("mod_next")
        out_specs.append(pl.BlockSpec((rows, tn), lambda bi, i: (0, bi * nt + i)))
        out_shape.append(jax.ShapeDtypeStruct((rows, 6 * D), F32))
        side = _cast_slabs(nxt, layer + 1, nt, steps)
        names, in_specs, args = names + side["names"], in_specs + side["in_specs"], args + side["args"]
        out_names, out_specs = out_names + side["out_names"], out_specs + side["out_specs"]
        out_shape = out_shape + side["out_shape"]
    return pl.pallas_call(
        functools.partial(_attn_kernel, names=tuple(names + out_names), tq=tq, prep_next=nxt is not None),
        grid=(b, nt),
        in_specs=in_specs,
        out_specs=out_specs,
        out_shape=out_shape,
        compiler_params=_params(52),
        name="attn_lat",
    )(*args)


def _post_kernel(*refs, names, final):
    r = dict(zip(names, refs))
    ypg = r["ypg"][0]
    ycat = jnp.concatenate([ypg[:, 0:POOL_W], r["ymla"][0], ypg[:, POOL_W:]], axis=1)
    r["o"][0] = _ffn_block(r["x"][0], ycat, r["mod"], r, r["g_final"][...] if final else None)


def _post(x, ypg, ymla, mod, ffn, g_final, *, tm):
    b, seq, _ = x.shape
    final = g_final is not None
    names = ["x", "ypg", "ymla", "mod"] + list(FFN_PARAMS)
    in_specs = [
        pl.BlockSpec((1, tm, D), lambda bi, i: (bi, i, 0)),
        pl.BlockSpec((1, tm, 2 * GMLP_W), lambda bi, i: (bi, i, 0)),
        pl.BlockSpec((1, tm, HEADS * V_DIM), lambda bi, i: (bi, i, 0)),
        _mod_spec(1, 1),
    ] + _param_specs(FFN_PARAMS, FFN_SHAPES, None)
    args = [x, ypg, ymla, mod] + [ffn[n] for n in FFN_PARAMS]
    if final:
        names.append("g_final")
        in_specs.append(pl.BlockSpec((1, D), lambda bi, i: (0, 0)))
        args.append(g_final)
    names.append("o")
    return pl.pallas_call(
        functools.partial(_post_kernel, names=tuple(names), final=final),
        grid=(b, seq // tm),
        in_specs=in_specs,
        out_specs=pl.BlockSpec((1, tm, D), lambda bi, i: (bi, i, 0)),
        out_shape=jax.ShapeDtypeStruct((b, seq, D), F32),
        compiler_params=_params(56),
        name="post_lat",
    )(*args)


def _rope_tables(seq):
    rows = seq // GRID_W
    row = jnp.repeat(jnp.arange(rows, dtype=F32), GRID_W)
    col = jnp.tile(jnp.arange(GRID_W, dtype=F32), rows)
    n_freq = QK_ROPE // 4
    inv = 1.0 / (ROPE_THETA ** (jnp.arange(n_freq, dtype=F32) / n_freq))
    ang = jnp.concatenate([row[:, None] * inv, col[:, None] * inv], axis=-1)
    cos, sin = jnp.cos(ang), jnp.sin(ang)
    z = jnp.zeros_like(cos)
    c = jnp.concatenate([cos, cos, cos, cos], axis=-1)
    s1 = jnp.concatenate([-sin, z, -sin, z], axis=-1)
    s2 = jnp.concatenate([z, sin, z, sin], axis=-1)
    return c, s1, s2


def kernel(x_prompt, x_sample, cache_ckv, cache_krope, c, c_ctx, w_ada, b_ada, g_mix, w_in, w_pool,
           pool_scale, g_q, w_uq, g_kv, w_ukv, g_sgu, w_s, b_s, w_out, g_ffn, w_ff1, w_ff2, g_final):
    dec_b = x_sample.shape[0]
    w_in_p, w_in_g = _win_prep(w_in)
    w_uq_h = w_uq.astype(BF16).reshape(DEPTH, Q_RANK, HEADS // 2, 2, QK_NOPE + QK_ROPE)
    w_uq_p = jnp.concatenate([w_uq_h[..., 0, :QK_NOPE], w_uq_h[..., 1, :QK_NOPE],
                              w_uq_h[..., 0, QK_NOPE:], w_uq_h[..., 1, QK_NOPE:]], axis=-1)
    w_uq_p = w_uq_p.reshape(DEPTH, Q_RANK, HEADS // 2 * Q_PAIR)
    w_ukv_h = w_ukv.astype(BF16).reshape(DEPTH, KV_RANK, HEADS, QK_NOPE + V_DIM)
    w_uk = w_ukv_h[..., :QK_NOPE].reshape(DEPTH, KV_RANK, HEADS * QK_NOPE)
    w_uv = w_ukv_h[..., QK_NOPE:].reshape(DEPTH, KV_RANK, HEADS * V_DIM)
    w_pool_bd = jnp.zeros((DEPTH, POOL_W, POOL_W), BF16)
    for g in range(len(POOL_WINDOWS)):
        sl = slice(g * POOL_GD, (g + 1) * POOL_GD)
        w_pool_bd = w_pool_bd.at[:, sl, sl].set(w_pool[:, g].astype(BF16))
    weights = {
        "g_mix": g_mix.reshape(DEPTH, 1, D),
        "w_in": w_in_p,
        "w_in_g": w_in_g,
        "g_q": (g_q * SM_SCALE).reshape(DEPTH, 1, Q_RANK),
        "w_uq": w_uq_p,
        "g_kv": g_kv.reshape(DEPTH, 1, KV_RANK),
        "w_uk": w_uk,
        "w_uv": w_uv,
        "w_pool": w_pool_bd,
        "pool_scale": pool_scale.reshape(DEPTH, 1, POOL_W),
        "g_sgu": g_sgu.reshape(DEPTH, 1, GMLP_W),
        "w_s": w_s.reshape(DEPTH, GMLP_G * CHUNK, CHUNK).astype(BF16),
        "b_s": jnp.repeat(jnp.swapaxes(b_s, 1, 2), GMLP_GD, axis=2),
    }
    g_fin = g_final.reshape(1, D)
    rope_tabs = _rope_tables(x_sample.shape[1])

    n_rows = 16
    c_all = jnp.zeros((n_rows, D), F32).at[0].set(c_ctx).at[1:1 + dec_b].set(c)
    b_ada3 = b_ada.reshape(DEPTH, 1, 6 * D)

    cache_kr_pad = jnp.pad(cache_krope, ((0, 0), (0, 0), (0, 0), (0, 128 - QK_ROPE)))
    cache = _cache_expand(cache_ckv, cache_kr_pad, w_uk, w_uv)

    mod = _ada_first(c_all, w_ada, b_ada3).reshape(n_rows, 1, 6 * D)
    f32_params = {"c_all": c_all, "w_ada": w_ada, "b_ada": b_ada3, "w_out": w_out, "w_ff1": w_ff1, "w_ff2": w_ff2}

    xp, xs = x_prompt, x_sample
    ctx_b, ctx_seq, _ = x_prompt.shape
    states = (jnp.zeros((ctx_b, DEPTH, ctx_seq, KV_RANK), F32), jnp.zeros((ctx_b, DEPTH, ctx_seq, QK_ROPE), F32))
    for l in range(DEPTH):
        last = l == DEPTH - 1
        g_last = g_fin if last else None
        q, k, v, ypg, *cast = _pre(xs, mod, l, weights, rope_tabs, f32_params if l == 0 else None, tm=1024)
        if l == 0:
            ffn = dict(zip(CAST_PARAMS, cast))
        ffn["g_ffn"] = g_ffn[l].reshape(1, D)
        xp, *states = _ctx_layer(xp, states, mod, l, weights, ffn, g_last)
        outs = _attn(q, k, v, cache, l, None if last else f32_params, tq=1024)
        xs = _post(xs, ypg, outs[0], mod, ffn, g_last, tm=1024)
        if not last:
            mod = outs[1].reshape(n_rows, 1, 6 * D)
            ffn = dict(zip(CAST_PARAMS, outs[2:]))
    return xp, xs, states[0], states[1]
```
